```python
import jax, jax.numpy as jnp
from jax import lax
import numpy as np

D_MODEL = 1024
BATCH = 16
SEQ = 4096
DEPTH = 1
DEC_BATCH = 128
DEC_SEQ = 1
PAST_LEN = 8192
PAGE_SIZE = 128

N_HEADS = 8
HEAD_DIM = 64
N_KV_HEADS = 2
GROUP = N_HEADS // N_KV_HEADS
ATTN_DIM = N_HEADS * HEAD_DIM
KV_DIM = N_KV_HEADS * HEAD_DIM
ATTN_SCALE = HEAD_DIM ** -0.5
CMP_LEN = 32
CMP_STRIDE = 16
CMP_SUB = CMP_LEN // CMP_STRIDE
CMP_HID = 4 * HEAD_DIM
SEL_BLOCK = 64
SEL_TOP = 16
FORCED_SCORE = 1e6
WINDOW = 512
Q_BLOCK = 64
CONV_DIM = D_MODEL // 2
CONV_WIDTH = 3
N_EXPERTS = 256
TOP_K = 8
N_GROUPS = 8
TOPK_GROUPS = 4
EXPERT_FF = D_MODEL // 4
SHARED_FF = D_MODEL // 4
ROUTED_SCALE = 2.5
MOE_BLOCK = 128
NORM_EPS = 1e-6
PROJ_SPLITS = (ATTN_DIM,) + (KV_DIM,) * 6 + (N_HEADS * 3,) + (CONV_DIM,) * 3 + (D_MODEL,) * 2
PROJ_DIM = sum(PROJ_SPLITS)

kernel_name = 'hybrid_nsa_shortconv_moe_decode_step'


def _rms(x, g):
    xf = x.astype(jnp.float32)
    y = xf * lax.rsqrt(jnp.mean(xf * xf, axis=-1, keepdims=True) + NORM_EPS)
    return (y * g.astype(jnp.float32)).astype(x.dtype)


def _masked_softmax(s, mask):
    s = jnp.where(mask, s, -jnp.inf)
    m = jnp.max(s, axis=-1, keepdims=True)
    m = jnp.where(jnp.isfinite(m), m, 0.0)
    e = jnp.where(mask, jnp.exp(s - m), 0.0)
    return e / jnp.maximum(jnp.sum(e, axis=-1, keepdims=True), 1e-30)


def _alibi_slopes():
    h = jnp.arange(1, N_HEADS + 1, dtype=jnp.float32)
    return (2.0 ** (-8.0 * h / N_HEADS)).reshape(1, 1, N_KV_HEADS, GROUP, 1)


def _compress(rows, pe, w1, w2):
    b, l = rows.shape[:2]
    n_chunks = l // CMP_STRIDE
    n_cmp = n_chunks - CMP_SUB + 1
    chunks = rows[:, :n_chunks * CMP_STRIDE].reshape(b, n_chunks, CMP_STRIDE, N_KV_HEADS, HEAD_DIM)
    w1_sub = w1.reshape(CMP_SUB, CMP_STRIDE, HEAD_DIM, CMP_HID)
    hid = jnp.einsum('pd,pdh->h', pe, w1)
    for j in range(CMP_SUB):
        hid = hid + jnp.einsum('bcsgd,sdh->bcgh', chunks[:, j:j + n_cmp], w1_sub[j])
    return jnp.einsum('bcgh,hd->bcgd', jax.nn.gelu(hid), w2)


def _nsa(q, t_pos, gates, kcmp, vcmp, n_ctx, gather_sel, kw, vw, kw_pos):
    b, t = q.shape[:2]
    qg = q.reshape(b, t, N_KV_HEADS, GROUP, HEAD_DIM)
    slopes = _alibi_slopes()
    tf = t_pos.astype(jnp.float32)

    def attend(scores, dist, mask, v, eq):
        p = _masked_softmax(scores.astype(jnp.float32) * ATTN_SCALE - slopes * dist, mask)
        return p, jnp.einsum(eq, p.astype(v.dtype), v)

    c_start = jnp.arange(kcmp.shape[1]) * CMP_STRIDE
    dist = (tf[:, None] - (c_start + (CMP_LEN - 1) / 2)[None, :])[None, :, None, None, :]
    mask = (c_start[None, :] + CMP_LEN - 1 <= t_pos[:, None])[None, :, None, None, :]
    p_cmp, o_cmp = attend(jnp.einsum('btgrd,bcgd->btgrc', qg, kcmp), dist, mask, vcmp, 'btgrc,bcgd->btgrd')

    n_slc = -(-n_ctx // SEL_BLOCK)
    sj = jnp.arange(n_slc)
    overlap = ((c_start[:, None] < (sj[None, :] + 1) * SEL_BLOCK)
               & (c_start[:, None] + CMP_LEN > sj[None, :] * SEL_BLOCK)).astype(jnp.float32)
    imp = jnp.einsum('btgc,cj->btgj', p_cmp.sum(axis=3), overlap)
    cur = (t_pos // SEL_BLOCK)[:, None]
    forced = (sj[None, :] == 0) | (sj[None, :] == cur) | (sj[None, :] == cur - 1)
    valid = sj[None, :] * SEL_BLOCK <= t_pos[:, None]
    imp = jnp.where(forced[None, :, None, :], FORCED_SCORE, imp)
    imp = jnp.where(valid[None, :, None, :], imp, -jnp.inf)
    n_top = min(SEL_TOP, n_slc)
    idx = lax.top_k(imp, n_top)[1]
    pos = (idx[..., None] * SEL_BLOCK + jnp.arange(SEL_BLOCK)).reshape(b, t, N_KV_HEADS, n_top * SEL_BLOCK)

    k_sel, v_sel = gather_sel(pos)
    dist = (tf[None, :, None, None] - pos.astype(jnp.float32))[:, :, :, None, :]
    mask = (pos <= t_pos[None, :, None, None])[:, :, :, None, :]
    _, o_sel = attend(jnp.einsum('btgrd,btgld->btgrl', qg, k_sel), dist, mask, v_sel, 'btgrl,btgld->btgrd')

    delta = t_pos[:, None] - kw_pos[None, :]
    mask = ((delta >= 0) & (delta <= WINDOW) & (kw_pos[None, :] >= 0))[None, :, None, None, :]
    _, o_win = attend(jnp.einsum('btgrd,blgd->btgrl', qg, kw), delta.astype(jnp.float32)[None, :, None, None, :],
                      mask, vw, 'btgrl,blgd->btgrd')

    g = gates.reshape(b, t, N_KV_HEADS, GROUP, 3)
    o = g[..., 0:1] * o_cmp + g[..., 1:2] * o_sel + g[..., 2:3] * o_win
    return o.reshape(b, t, ATTN_DIM)


def _nsa_prompt(q, kc, vc, ks, vs, kw, vw, g_nsa, cmp_w):
    cmp_pe, cmp_w1, cmp_w2, g_kcmp = cmp_w
    b, s = q.shape[:2]
    kcmp = _rms(_compress(kc, cmp_pe[0], cmp_w1[0], cmp_w2[0]), g_kcmp)
    vcmp = _compress(vc, cmp_pe[1], cmp_w1[1], cmp_w2[1])
    bidx = jnp.arange(b)[:, None, None, None]
    gidx = jnp.arange(N_KV_HEADS)[None, None, :, None]

    def gather_sel(pos):
        return ks[bidx, pos, gidx], vs[bidx, pos, gidx]

    pad = ((0, 0), (WINDOW, 0), (0, 0), (0, 0))
    kw_pad, vw_pad = jnp.pad(kw, pad), jnp.pad(vw, pad)
    n_win = WINDOW + Q_BLOCK

    def q_block(i):
        t0 = i * Q_BLOCK
        cut = lambda a, n: lax.dynamic_slice_in_dim(a, t0, n, axis=1)
        return _nsa(cut(q, Q_BLOCK), t0 + jnp.arange(Q_BLOCK), cut(g_nsa, Q_BLOCK), kcmp, vcmp, s, gather_sel,
                    cut(kw_pad, n_win), cut(vw_pad, n_win), t0 - WINDOW + jnp.arange(n_win))

    o = lax.map(q_block, jnp.arange(s // Q_BLOCK))
    return jnp.moveaxis(o, 0, 1).reshape(b, s, ATTN_DIM)


def _nsa_sample(q, kc, vc, ks, vs, kw, vw, g_nsa, cmp_w, cache_kc, cache_vc, cache_ks, cache_vs,
                buf_kw, buf_vw, layer, page_table):
    cmp_pe, cmp_w1, cmp_w2, g_kcmp = cmp_w
    db, t = q.shape[:2]
    past = page_table.shape[1] * PAGE_SIZE

    def paged_rows(pool):
        return pool[layer, page_table].reshape(db, past, N_KV_HEADS, HEAD_DIM)

    kc_ctx = jnp.concatenate([paged_rows(cache_kc), kc], axis=1)
    vc_ctx = jnp.concatenate([paged_rows(cache_vc), vc], axis=1)
    kcmp = _rms(_compress(kc_ctx, cmp_pe[0], cmp_w1[0], cmp_w2[0]), g_kcmp)
    vcmp = _compress(vc_ctx, cmp_pe[1], cmp_w1[1], cmp_w2[1])
    bidx = jnp.arange(db)[:, None, None, None]
    gidx = jnp.arange(N_KV_HEADS)[None, None, :, None]

    def gather_sel(pos):
        pc = jnp.minimum(pos, past - 1)
        page = page_table[bidx, pc // PAGE_SIZE]
        row = pc % PAGE_SIZE
        new_row = jnp.clip(pos - past, 0, t - 1)
        is_past = (pos < past)[..., None]

        def take(pool, new):
            return jnp.where(is_past, pool[layer, page, row, gidx], new[bidx, new_row, gidx])
        return take(cache_ks, ks), take(cache_vs, vs)

    wbuf = buf_kw.shape[1]
    kw_ctx = jnp.concatenate([buf_kw, kw], axis=1)
    vw_ctx = jnp.concatenate([buf_vw, vw], axis=1)
    o = _nsa(q, past + jnp.arange(t), g_nsa, kcmp, vcmp, past + t, gather_sel, kw_ctx, vw_ctx,
             past - wbuf + jnp.arange(wbuf + t))
    return o, kw_ctx[:, -wbuf:], vw_ctx[:, -wbuf:]


def _in_proj(h, w_in, qk_g):
    b, t = h.shape[:2]
    parts = jnp.split(h @ w_in, np.cumsum(PROJ_SPLITS)[:-1].tolist(), axis=-1)
    q, kc, vc, ks, vs, kw, vw, g_nsa, x_in, b_gate, c_gate, m_att, m_conv = parts
    heads = lambda a: a.reshape(b, t, -1, HEAD_DIM)
    return (_rms(heads(q), qk_g[0]), heads(kc), heads(vc), _rms(heads(ks), qk_g[2]), heads(vs),
            _rms(heads(kw), qk_g[3]), heads(vw), jax.nn.sigmoid(g_nsa).reshape(b, t, N_HEADS, 3),
            c_gate * x_in, b_gate, jax.nn.sigmoid(m_att), jax.nn.sigmoid(m_conv))


def _short_conv(u_ext, w):
    t = u_ext.shape[1] - (CONV_WIDTH - 1)
    y = u_ext[:, :t] * w[0]
    for j in range(1, CONV_WIDTH):
        y = y + u_ext[:, j:j + t] * w[j]
    return y


def _merge(o_att, y_conv, m_att, m_conv, w_attn_out, w_conv_out, w_o):
    return (m_att * (o_att @ w_attn_out) + m_conv * (y_conv @ w_conv_out)) @ w_o


def _swiglu(x, w_gate, w_up, w_down):
    return (jax.nn.silu(x @ w_gate) * (x @ w_up)) @ w_down


def _routed_experts(hf, e_idx, e_wt, w_gate, w_up, w_down):
    n = hf.shape[0]
    n_assign = n * TOP_K
    flat_e = e_idx.reshape(-1)
    order = jnp.argsort(flat_e)
    e_sorted = flat_e[order]
    tok_sorted = (order // TOP_K).astype(jnp.int32)
    wt_sorted = e_wt.reshape(-1)[order]
    counts = jnp.bincount(flat_e, length=N_EXPERTS)
    padded = (counts + MOE_BLOCK - 1) // MOE_BLOCK * MOE_BLOCK
    pad_end = jnp.cumsum(padded)
    start = jnp.cumsum(counts) - counts
    dest = (pad_end - padded)[e_sorted] + jnp.arange(n_assign) - start[e_sorted]
    n_blocks = -(-n_assign // MOE_BLOCK) + N_EXPERTS
    n_rows = n_blocks * MOE_BLOCK
    row_tok = jnp.full((n_rows,), n, jnp.int32).at[dest].set(tok_sorted)
    row_wt = jnp.zeros((n_rows,), jnp.float32).at[dest].set(wt_sorted)
    block_e = jnp.minimum(jnp.searchsorted(pad_end, jnp.arange(n_blocks) * MOE_BLOCK, side='right'), N_EXPERTS - 1)
    h_pad = jnp.concatenate([hf, jnp.zeros((1, D_MODEL), hf.dtype)], axis=0)

    def expert_block(args):
        tok, e = args
        return _swiglu(h_pad[tok], w_gate[e], w_up[e], w_down[e])

    out = lax.map(expert_block, (row_tok.reshape(n_blocks, MOE_BLOCK), block_e)).reshape(n_rows, D_MODEL)
    y = jnp.zeros((n + 1, D_MODEL), out.dtype).at[row_tok].add(out * row_wt[:, None].astype(out.dtype))
    return y[:n]


def _moe(h, moe_w):
    router_w, router_b, e_gate, e_up, e_down, s_gate, s_up, s_down = moe_w
    lead = h.shape[:-1]
    hf = h.reshape(-1, D_MODEL)
    n = hf.shape[0]
    aff = jax.nn.sigmoid((hf @ router_w).astype(jnp.float32))
    biased = aff + router_b.astype(jnp.float32)
    grp_score = lax.top_k(biased.reshape(n, N_GROUPS, -1), 2)[0].sum(-1)
    grp_keep = jax.nn.one_hot(lax.top_k(grp_score, TOPK_GROUPS)[1], N_GROUPS).sum(1) > 0
    cand = jnp.where(jnp.repeat(grp_keep, N_EXPERTS // N_GROUPS, axis=1), biased, -jnp.inf)
    e_idx = lax.top_k(cand, TOP_K)[1]
    w = jnp.take_along_axis(aff, e_idx, axis=1)
    w = w / jnp.sum(w, axis=-1, keepdims=True) * ROUTED_SCALE
    y = _routed_experts(hf, e_idx, w, e_gate, e_up, e_down) + _swiglu(hf, s_gate, s_up, s_down)
    return y.reshape(*lead, D_MODEL)


def _adaln(c, w_ada, b_ada):
    return jnp.split(jax.nn.silu(c) @ w_ada + b_ada, 6, axis=-1)


def _modulate(x, g, shift, scale):
    return _rms(x, g) * (1.0 + scale[:, None]) + shift[:, None]


def _layer(x_p, x_s, c_p, c_s, layer, caches, page_table, mod_w, mix_w, moe_w):
    ckc, cvc, cks, cvs, ckw, cvw, sconv = caches
    w_ada, b_ada, norm1_g, norm2_g = mod_w
    w_in, qk_g, cmp_pe, cmp_w1, cmp_w2, conv_w, w_attn_out, w_conv_out, w_o = mix_w
    cmp_w = (cmp_pe, cmp_w1, cmp_w2, qk_g[1])
    hist = CONV_WIDTH - 1

    ada = _adaln(c_p, w_ada, b_ada)
    q, kc, vc, ks, vs, kw, vw, g_nsa, u, b_gate, m_att, m_conv = _in_proj(
        _modulate(x_p, norm1_g, ada[0], ada[1]), w_in, qk_g)
    o_att = _nsa_prompt(q, kc, vc, ks, vs, kw, vw, g_nsa, cmp_w)
    y_conv = b_gate * _short_conv(jnp.pad(u, ((0, 0), (hist, 0), (0, 0))), conv_w)
    x_p = x_p + ada[2][:, None] * _merge(o_att, y_conv, m_att, m_conv, w_attn_out, w_conv_out, w_o)
    x_p = x_p + ada[5][:, None] * _moe(_modulate(x_p, norm2_g, ada[3], ada[4]), moe_w)
    keep = min(WINDOW, x_p.shape[1])
    p_state = (kc, vc, ks, vs, kw[:, -keep:], vw[:, -keep:], u[:, -hist:])

    ada = _adaln(c_s, w_ada, b_ada)
    q, kc, vc, ks, vs, kw, vw, g_nsa, u, b_gate, m_att, m_conv = _in_proj(
        _modulate(x_s, norm1_g, ada[0], ada[1]), w_in, qk_g)
    o_att, kw_buf, vw_buf = _nsa_sample(q, kc, vc, ks, vs, kw, vw, g_nsa, cmp_w, ckc, cvc, cks, cvs,
                                        ckw[layer], cvw[layer], layer, page_table)
    u_ext = jnp.concatenate([sconv[layer], u], axis=1)
    y_conv = b_gate * _short_conv(u_ext, conv_w)
    x_s = x_s + ada[2][:, None] * _merge(o_att, y_conv, m_att, m_conv, w_attn_out, w_conv_out, w_o)
    x_s = x_s + ada[5][:, None] * _moe(_modulate(x_s, norm2_g, ada[3], ada[4]), moe_w)
    s_state = (kc, vc, ks, vs, kw_buf, vw_buf, u_ext[:, -hist:])
    return x_p, x_s, p_state + s_state


def setup_inputs(seed: int = 0) -> dict:
    key = jax.random.key(seed)
    keys = iter(jax.random.split(key, 48))

    def nrm(shape, scale=1.0):
        return jax.random.normal(next(keys), shape, jnp.float32) * scale

    def gain(shape):
        return 1.0 + nrm(shape, 0.05)

    n_pages = PAST_LEN // PAGE_SIZE
    n_used = DEC_BATCH * n_pages
    n_pool = n_used + n_used // 4
    wbuf = min(WINDOW, PAST_LEN)
    pool_shape = (DEPTH, n_pool, PAGE_SIZE, N_KV_HEADS, HEAD_DIM)
    win_shape = (DEPTH, DEC_BATCH, wbuf, N_KV_HEADS, HEAD_DIM)
    page_table = jax.random.permutation(next(keys), n_pool)[:n_used].reshape(DEC_BATCH, n_pages).astype(jnp.int32)
    return {
        'x_prompt': nrm((BATCH, SEQ, D_MODEL)),
        'x_sample': nrm((DEC_BATCH, DEC_SEQ, D_MODEL)),
        'cache_k_cmp': nrm(pool_shape),
        'cache_v_cmp': nrm(pool_shape),
        'cache_k_sel': nrm(pool_shape),
        'cache_v_sel': nrm(pool_shape),
        'cache_k_win': nrm(win_shape),
        'cache_v_win': nrm(win_shape),
        'state_conv': nrm((DEPTH, DEC_BATCH, CONV_WIDTH - 1, CONV_DIM)),
        'page_table': page_table,
        'c_prompt': nrm((BATCH, D_MODEL)),
        'c_sample': nrm((DEC_BATCH, D_MODEL)),
        'w_ada': nrm((DEPTH, D_MODEL, 6 * D_MODEL), 0.5 * D_MODEL ** -0.5),
        'b_ada': nrm((DEPTH, 6 * D_MODEL), 0.02),
        'norm1_g': gain((DEPTH, D_MODEL)),
        'norm2_g': gain((DEPTH, D_MODEL)),
        'w_in': nrm((DEPTH, D_MODEL, PROJ_DIM), D_MODEL ** -0.5),
        'qk_norm_g': gain((DEPTH, 4, HEAD_DIM)),
        'cmp_pe': nrm((DEPTH, 2, CMP_LEN, HEAD_DIM), 0.2),
        'cmp_w1': nrm((DEPTH, 2, CMP_LEN, HEAD_DIM, CMP_HID), (CMP_LEN * HEAD_DIM) ** -0.5),
        'cmp_w2': nrm((DEPTH, 2, CMP_HID, HEAD_DIM), CMP_HID ** -0.5),
        'conv_w': nrm((DEPTH, CONV_WIDTH, CONV_DIM), CONV_WIDTH ** -0.5),
        'w_attn_out': nrm((DEPTH, ATTN_DIM, D_MODEL), ATTN_DIM ** -0.5),
        'w_conv_out': nrm((DEPTH, CONV_DIM, D_MODEL), CONV_DIM ** -0.5),
        'w_o': nrm((DEPTH, D_MODEL, D_MODEL), D_MODEL ** -0.5),
        'router_w': nrm((DEPTH, D_MODEL, N_EXPERTS), D_MODEL ** -0.5),
        'router_b': nrm((DEPTH, N_EXPERTS), 0.01),
        'exp_w_gate': nrm((DEPTH, N_EXPERTS, D_MODEL, EXPERT_FF), D_MODEL ** -0.5),
        'exp_w_up': nrm((DEPTH, N_EXPERTS, D_MODEL, EXPERT_FF), D_MODEL ** -0.5),
        'exp_w_down': nrm((DEPTH, N_EXPERTS, EXPERT_FF, D_MODEL), EXPERT_FF ** -0.5),
        'shared_w_gate': nrm((DEPTH, D_MODEL, SHARED_FF), D_MODEL ** -0.5),
        'shared_w_up': nrm((DEPTH, D_MODEL, SHARED_FF), D_MODEL ** -0.5),
        'shared_w_down': nrm((DEPTH, SHARED_FF, D_MODEL), SHARED_FF ** -0.5),
    }


def reference(x_prompt, x_sample, cache_k_cmp, cache_v_cmp, cache_k_sel, cache_v_sel, cache_k_win, cache_v_win,
              state_conv, page_table, c_prompt, c_sample, w_ada, b_ada, norm1_g, norm2_g, w_in, qk_norm_g,
              cmp_pe, cmp_w1, cmp_w2, conv_w, w_attn_out, w_conv_out, w_o, router_w, router_b,
              exp_w_gate, exp_w_up, exp_w_down, shared_w_gate, shared_w_up, shared_w_down):
    x_p, x_s = x_prompt, x_sample
    caches = (cache_k_cmp, cache_v_cmp, cache_k_sel, cache_v_sel, cache_k_win, cache_v_win, state_conv)
    per_layer = []
    for l in range(DEPTH):
        x_p, x_s, st = _layer(
            x_p, x_s, c_prompt, c_sample, l, caches, page_table,
            (w_ada[l], b_ada[l], norm1_g[l], norm2_g[l]),
            (w_in[l], qk_norm_g[l], cmp_pe[l], cmp_w1[l], cmp_w2[l], conv_w[l], w_attn_out[l], w_conv_out[l], w_o[l]),
            (router_w[l], router_b[l], exp_w_gate[l], exp_w_up[l], exp_w_down[l],
             shared_w_gate[l], shared_w_up[l], shared_w_down[l]))
        per_layer.append(st)
    (p_k_cmp, p_v_cmp, p_k_sel, p_v_sel, p_k_win, p_v_win, p_conv,
     s_k_cmp, s_v_cmp, s_k_sel, s_v_sel, s_k_win, s_v_win, s_conv) = [jnp.stack(s) for s in zip(*per_layer)]
    return (x_p, x_s, p_k_cmp, p_v_cmp, p_k_sel, p_v_sel, p_k_win, p_v_win, p_conv,
            s_k_cmp, s_v_cmp, s_k_sel, s_v_sel, s_k_win, s_v_win, s_conv)
```

```python
import functools

import jax
import jax.numpy as jnp
from jax import lax
from jax.experimental import pallas as pl
from jax.experimental.pallas import tpu as pltpu

F32 = jnp.float32
BF16 = jnp.bfloat16
I32 = jnp.int32

D_MODEL = 1024
N_HEADS = 8
HEAD_DIM = 64
N_KV_HEADS = 2
GROUP = N_HEADS // N_KV_HEADS
ATTN_DIM = N_HEADS * HEAD_DIM
KV_DIM = N_KV_HEADS * HEAD_DIM
ATTN_SCALE = HEAD_DIM ** -0.5
CMP_LEN = 32
CMP_STRIDE = 16
CMP_HID = 4 * HEAD_DIM
SEL_BLOCK = 64
SEL_TOP = 16
FORCED_SCORE = 1e6
WINDOW = 512
CONV_DIM = D_MODEL // 2
CONV_WIDTH = 3
N_EXPERTS = 256
TOP_K = 8
N_GROUPS = 8
TOPK_GROUPS = 4
EXPERT_FF = D_MODEL // 4
ROUTED_SCALE = 2.5
NORM_EPS = 1e-6
PAGE_ROWS = 128
CHUNKS_PER_PAGE = PAGE_ROWS // CMP_STRIDE
NEG = -1e30
GATE_PAD = 128
MOE_ROWS = 256
VMEM_LIMIT = 56 * 1024 * 1024


def _params(sem):
    return pltpu.CompilerParams(dimension_semantics=sem, vmem_limit_bytes=VMEM_LIMIT)


def _bdot(a, b):
    return jnp.dot(a.astype(BF16), b.astype(BF16), preferred_element_type=F32)


def _bdot_nt(a, b):
    return lax.dot_general(a.astype(BF16), b.astype(BF16), (((1,), (1,)), ((), ())),
                           preferred_element_type=F32)


def _split_dot(a, b_bf16, nt=False):
    hi = a.astype(BF16)
    lo = (a - hi.astype(F32)).astype(BF16)
    f = _bdot_nt if nt else _bdot
    return f(hi, b_bf16) + f(lo, b_bf16)


def _sigmoid(x):
    return 1.0 / (1.0 + jnp.exp(-x))


def _silu(x):
    return x * _sigmoid(x)


def _gelu_tanh(x):
    return 0.5 * x * (1.0 + jnp.tanh(0.7978845608028654 * (x + 0.044715 * (x * x * x))))


def _head_rms(z, gain):
    r = lax.broadcasted_iota(I32, (128, 128), 0) // HEAD_DIM
    c = lax.broadcasted_iota(I32, (128, 128), 1) // HEAD_DIM
    seg = (r == c).astype(BF16)
    x2 = z * z
    parts = [_split_dot(x2[:, j:j + 128], seg) for j in range(0, z.shape[1], 128)]
    ss = parts[0] if len(parts) == 1 else jnp.concatenate(parts, axis=1)
    return z * lax.rsqrt(ss * (1.0 / HEAD_DIM) + NORM_EPS) * gain


def _rms_mod(x, g, shift, scale):
    y = x * lax.rsqrt(jnp.mean(x * x, axis=-1, keepdims=True) + NORM_EPS) * g
    return y * (1.0 + scale) + shift


def _ada_kernel(c_ref, w_ref, b_ref, o_ref):
    o_ref[...] = _bdot(_silu(c_ref[...]), w_ref[...]) + b_ref[...]


def _ada(c, w, b):
    m, d = c.shape
    n = w.shape[1]
    tn = 1024
    return pl.pallas_call(
        _ada_kernel,
        grid=(n // tn,),
        in_specs=[pl.BlockSpec((m, d), lambda j: (0, 0)),
                  pl.BlockSpec((d, tn), lambda j: (0, j)),
                  pl.BlockSpec((1, tn), lambda j: (0, j))],
        out_specs=pl.BlockSpec((m, tn), lambda j: (0, j)),
        out_shape=jax.ShapeDtypeStruct((m, n), F32),
        compiler_params=_params(("arbitrary",)),
        name="ada_ln",
    )(c, w, b.reshape(1, n))


_C_Q = 0
_C_KV = ATTN_DIM
_C_G = _C_KV + 6 * KV_DIM
_C_X = _C_G + GATE_PAD
_C_B = _C_X + CONV_DIM
_C_C = _C_B + CONV_DIM
_C_MA = _C_C + CONV_DIM
_C_MC = _C_MA + D_MODEL
_C_END = _C_MC + D_MODEL


def _inproj_kernel(x_ref, shift_ref, scale_ref, g1_ref, w_ref, gq_ref, gks_ref, gkw_ref,
                   q_ref, kc_ref, vc_ref, ks_ref, vs_ref, kw_ref, vw_ref, gn_ref,
                   u_ref, bg_ref, ma_ref, mc_ref):
    h = _rms_mod(x_ref[0], g1_ref[...], shift_ref[0], scale_ref[0]).astype(BF16)

    def proj(c0, n):
        return jnp.dot(h, w_ref[:, c0:c0 + n], preferred_element_type=F32)

    q_ref[0] = _head_rms(proj(_C_Q, ATTN_DIM), gq_ref[...])
    kc_ref[0] = proj(_C_KV, KV_DIM)
    vc_ref[0] = proj(_C_KV + KV_DIM, KV_DIM)
    ks_ref[0] = _head_rms(proj(_C_KV + 2 * KV_DIM, KV_DIM), gks_ref[...])
    vs_ref[0] = proj(_C_KV + 3 * KV_DIM, KV_DIM)
    kw_ref[0] = _head_rms(proj(_C_KV + 4 * KV_DIM, KV_DIM), gkw_ref[...])
    vw_ref[0] = proj(_C_KV + 5 * KV_DIM, KV_DIM)
    gn_ref[0] = _sigmoid(proj(_C_G, GATE_PAD))
    u_ref[0] = proj(_C_C, CONV_DIM) * proj(_C_X, CONV_DIM)
    bg_ref[0] = proj(_C_B, CONV_DIM)
    ma_ref[0] = _sigmoid(proj(_C_MA, D_MODEL))
    mc_ref[0] = _sigmoid(proj(_C_MC, D_MODEL))


def _in_proj(x, shift, scale, g1, w_packed, gq, gks, gkw, tm):
    b, s, d = x.shape
    tm = min(tm, s)
    r = shift.shape[1]
    rb = 1 if r == 1 else tm
    mod_map = (lambda i, j: (i, 0, 0)) if r == 1 else (lambda i, j: (i, j, 0))
    row = lambda n: pl.BlockSpec((1, tm, n), lambda i, j: (i, j, 0))
    const = lambda a: pl.BlockSpec(a.shape, lambda i, j: (0, 0))
    widths = (ATTN_DIM,) + (KV_DIM,) * 6 + (GATE_PAD, CONV_DIM, CONV_DIM, D_MODEL, D_MODEL)
    return pl.pallas_call(
        _inproj_kernel,
        grid=(b, s // tm),
        in_specs=[row(d), pl.BlockSpec((1, rb, d), mod_map), pl.BlockSpec((1, rb, d), mod_map),
                  const(g1), const(w_packed), const(gq), const(gks), const(gkw)],
        out_specs=[row(n) for n in widths],
        out_shape=[jax.ShapeDtypeStruct((b, s, n), F32) for n in widths],
        compiler_params=_params(("arbitrary", "arbitrary")),
        name="in_proj",
    )(x, shift, scale, g1, w_packed, gq, gks, gkw)


def _compress_kernel(pt_ref, kpool, vpool, w1k_ref, w1v_ref, pe_ref, w2k_ref, w2v_ref, gk_ref,
                     kcmp_ref, vcmp_ref, kbuf, vbuf, sem, *, n_pages):
    b = pl.program_id(0)
    nb = pl.num_programs(0)
    slot = b % 2

    def page_copy(pool, buf, seq, p, sl, which):
        return pltpu.make_async_copy(pool.at[pt_ref[seq, p]], buf.at[sl, p], sem.at[which, sl])

    def fetch(seq, sl):
        for p in range(n_pages):
            page_copy(kpool, kbuf, seq, p, sl, 0).start()
            page_copy(vpool, vbuf, seq, p, sl, 1).start()

    @pl.when(b == 0)
    def _():
        fetch(0, 0)

    @pl.when(b + 1 < nb)
    def _():
        fetch(b + 1, 1 - slot)

    for p in range(n_pages):
        page_copy(kpool, kbuf, b, p, slot, 0).wait()
        page_copy(vpool, vbuf, b, p, slot, 1).wait()

    n_chunk = n_pages * CHUNKS_PER_PAGE
    half = N_KV_HEADS * CMP_HID

    def summarise(buf, w1_ref, w2_ref, pe0, pe1):
        x = buf[slot].reshape(n_chunk, buf.shape[-1]).astype(BF16)
        a = jnp.dot(x, w1_ref[...], preferred_element_type=F32)
        pe_term = (jnp.dot(pe0.astype(BF16), w1_ref[:, :half], preferred_element_type=F32)
                   + jnp.dot(pe1.astype(BF16), w1_ref[:, half:], preferred_element_type=F32))
        nxt = pltpu.roll(a[:, half:], n_chunk - 1, 0)
        hid = a[:, :half] + nxt + pe_term[0:1]
        return jnp.dot(_gelu_tanh(hid).astype(BF16), w2_ref[...], preferred_element_type=F32)

    kcmp_ref[0] = _head_rms(summarise(kbuf, w1k_ref, w2k_ref, pe_ref[0], pe_ref[1]), gk_ref[...])
    vcmp_ref[0] = summarise(vbuf, w1v_ref, w2v_ref, pe_ref[2], pe_ref[3])


def _compress(page_table, kpool, vpool, cw):
    n_seq, n_pages = page_table.shape
    n_chunk = n_pages * CHUNKS_PER_PAGE
    width = CMP_STRIDE * KV_DIM
    kp = kpool.reshape(-1, CHUNKS_PER_PAGE, width)
    vp = vpool.reshape(-1, CHUNKS_PER_PAGE, width)
    const = lambda a: pl.BlockSpec(a.shape, lambda i, pt: (0,) * a.ndim)
    out = pl.BlockSpec((1, n_chunk, KV_DIM), lambda i, pt: (i, 0, 0))
    grid_spec = pltpu.PrefetchScalarGridSpec(
        num_scalar_prefetch=1,
        grid=(n_seq,),
        in_specs=[pl.BlockSpec(memory_space=pl.ANY), pl.BlockSpec(memory_space=pl.ANY),
                  const(cw["w1k"]), const(cw["w1v"]), const(cw["pe"]),
                  const(cw["w2k"]), const(cw["w2v"]), const(cw["gk"])],
        out_specs=[out, out],
        scratch_shapes=[pltpu.VMEM((2, n_pages, CHUNKS_PER_PAGE, width), F32),
                        pltpu.VMEM((2, n_pages, CHUNKS_PER_PAGE, width), F32),
                        pltpu.SemaphoreType.DMA((2, 2))],
    )
    return pl.pallas_call(
        functools.partial(_compress_kernel, n_pages=n_pages),
        grid_spec=grid_spec,
        out_shape=[jax.ShapeDtypeStruct((n_seq, n_chunk, KV_DIM), F32)] * 2,
        compiler_params=_params(("arbitrary",)),
        name="compress",
    )(page_table, kp, vp, cw["w1k"], cw["w1v"], cw["pe"], cw["w2k"], cw["w2v"], cw["gk"])


def _group_queries(q, g):
    parts = []
    for r in range(GROUP):
        h = GROUP * g + r
        qh = q[:, HEAD_DIM * h:HEAD_DIM * (h + 1)]
        z = jnp.zeros_like(qh)
        parts.append(jnp.concatenate([qh, z] if g == 0 else [z, qh], axis=1))
    return jnp.concatenate(parts, axis=0)


def _stack_cols(cols):
    return jnp.concatenate(cols, axis=0)


def _overlap(n_cmp, n_slc):
    c = lax.broadcasted_iota(I32, (n_cmp, n_slc), 0) * CMP_STRIDE
    j = lax.broadcasted_iota(I32, (n_cmp, n_slc), 1) * SEL_BLOCK
    return ((c < j + SEL_BLOCK) & (c + CMP_LEN > j)).astype(BF16)


def _nsa_prompt_kernel(q_ref, kcmp_ref, vcmp_ref, ks_ref, vs_ref, kw_ref, vw_ref, gn_ref, o_ref,
                       *, tq, n_slc):
    tk = tq
    i = pl.program_id(1)
    t0 = i * tq
    q = q_ref[0] * ATTN_SCALE
    gn = gn_ref[0]
    n_cmp = kcmp_ref.shape[1]
    rows = GROUP * tq
    t_loc = lax.broadcasted_iota(I32, (tq, 1), 0)
    t_row = t0 + t_loc
    t_rows = _stack_cols([t_row] * GROUP)
    ci = (lax.broadcasted_iota(I32, (rows, tk), 1)
          - _stack_cols([lax.broadcasted_iota(I32, (tq, tk), 0)] * GROUP))
    n_top = min(SEL_TOP, n_slc)
    outs = [None] * N_HEADS

    for g in range(N_KV_HEADS):
        qg = _group_queries(q, g).astype(BF16)
        slope_col = _stack_cols([jnp.full((tq, 1), 2.0 ** -(GROUP * g + r + 1), F32)
                                 for r in range(GROUP)])
        slope_b = jnp.broadcast_to(slope_col, (rows, tk))

        sc = _bdot_nt(qg, kcmp_ref[0])
        c_start = lax.broadcasted_iota(I32, (1, n_cmp), 1) * CMP_STRIDE
        dist = t_rows.astype(F32) - (c_start.astype(F32) + (CMP_LEN - 1) / 2)
        vis = c_start + (CMP_LEN - 1) <= t_rows
        sc = jnp.where(vis, sc - slope_col * dist, NEG)
        m = jnp.max(sc, axis=-1, keepdims=True)
        e = jnp.where(vis, jnp.exp(sc - m), 0.0)
        p = e / jnp.maximum(jnp.sum(e, axis=-1, keepdims=True), 1e-30)
        o_cmp = _bdot(p, vcmp_ref[0])

        psum = p[0:tq]
        for r in range(1, GROUP):
            psum = psum + p[r * tq:(r + 1) * tq]
        imp = _split_dot(psum, _overlap(n_cmp, n_slc))
        j_row = lax.broadcasted_iota(I32, (1, n_slc), 1)
        cur = t_row // SEL_BLOCK
        forced = (j_row == 0) | (j_row == cur) | (j_row == cur - 1)
        valid = j_row * SEL_BLOCK <= t_row
        imp = jnp.where(forced, FORCED_SCORE, imp)
        imp = jnp.where(valid, imp, -1.0)
        rank = jnp.zeros((tq, n_slc), F32)
        for b in range(n_slc):
            col = imp[:, b:b + 1]
            beats = (col > imp) | ((col == imp) & (j_row > b))
            rank = rank + beats.astype(F32)
        sel = (rank < n_top) & valid
        sel_neg = jnp.where(sel, 0.0, NEG)
        sel_neg = _stack_cols([sel_neg] * GROUP).astype(BF16)

        def flash(k_ref, v_ref, lo, hi, use_sel):
            def body(kt, carry):
                m_i, l_i, acc = carry
                s0 = pl.multiple_of(kt * tk, tk)
                k = k_ref[0, pl.ds(s0, tk), :]
                v = v_ref[0, pl.ds(s0, tk), :]
                s = _bdot_nt(qg, k)
                di = ci - (t0 - s0)
                s = s + slope_b * di.astype(F32)
                if use_sel:
                    blk = (s0 + lax.broadcasted_iota(I32, (n_slc, tk), 1)) // SEL_BLOCK
                    onehot = (lax.broadcasted_iota(I32, (n_slc, tk), 0) == blk).astype(BF16)
                    s = s + jnp.dot(sel_neg, onehot, preferred_element_type=F32)
                    s = jnp.where(di > 0, NEG, s)
                else:
                    s = jnp.where((di > 0) | (di < -WINDOW), NEG, s)
                m_new = jnp.maximum(m_i, jnp.max(s, axis=-1, keepdims=True))
                alpha = jnp.exp(m_i - m_new)
                pe = jnp.exp(s - m_new)
                l_new = alpha * l_i + jnp.sum(pe, axis=-1, keepdims=True)
                acc = alpha * acc + _bdot(pe, v)
                return m_new, l_new, acc

            init = (jnp.full((rows, 1), NEG, F32), jnp.zeros((rows, 1), F32),
                    jnp.zeros((rows, KV_DIM), F32))
            _, l_f, acc = lax.fori_loop(lo, hi, body, init)
            return acc / jnp.maximum(l_f, 1e-30)

        o_sel = flash(ks_ref, vs_ref, 0, i + 1, True)
        o_win = flash(kw_ref, vw_ref, jnp.maximum(i - WINDOW // tk, 0), i + 1, False)

        gate = lambda j: _stack_cols([gn[:, 3 * (GROUP * g + r) + j:3 * (GROUP * g + r) + j + 1]
                                      for r in range(GROUP)])
        og = gate(0) * o_cmp + gate(1) * o_sel + gate(2) * o_win
        for r in range(GROUP):
            outs[GROUP * g + r] = og[r * tq:(r + 1) * tq, HEAD_DIM * g:HEAD_DIM * (g + 1)]

    o_ref[0] = jnp.concatenate(outs, axis=1)


def _nsa_prompt(q, kcmp, vcmp, ks, vs, kw, vw, gn, tq):
    b, s, _ = q.shape
    tq = min(tq, s)
    assert s % tq == 0 and WINDOW % tq == 0 and tq % SEL_BLOCK == 0
    n_slc = -(-s // SEL_BLOCK)
    n_cmp = kcmp.shape[1]
    tile = lambda n: pl.BlockSpec((1, tq, n), lambda i, j: (i, j, 0))
    seq = lambda n, w: pl.BlockSpec((1, n, w), lambda i, j: (i, 0, 0))
    return pl.pallas_call(
        functools.partial(_nsa_prompt_kernel, tq=tq, n_slc=n_slc),
        grid=(b, s // tq),
        in_specs=[tile(ATTN_DIM), seq(n_cmp, KV_DIM), seq(n_cmp, KV_DIM),
                  seq(s, KV_DIM), seq(s, KV_DIM), seq(s, KV_DIM), seq(s, KV_DIM), tile(GATE_PAD)],
        out_specs=tile(ATTN_DIM),
        out_shape=jax.ShapeDtypeStruct((b, s, ATTN_DIM), F32),
        compiler_params=_params(("arbitrary", "arbitrary")),
        name="nsa_prompt",
    )(q, kcmp, vcmp, ks, vs, kw, vw, gn)


def _query_rows(q):
    rows = []
    for h in range(N_HEADS):
        qh = q[:, HEAD_DIM * h:HEAD_DIM * (h + 1)]
        z = jnp.zeros_like(qh)
        rows.append(jnp.concatenate([qh, z] if h < GROUP else [z, qh], axis=1))
    return jnp.concatenate(rows, axis=0)


def _head_slopes():
    return jnp.concatenate([jnp.full((1, 1), 2.0 ** -(h + 1), F32) for h in range(N_HEADS)], axis=0)


def _per_group(rows_iota, v0, v1):
    return jnp.where(rows_iota < GROUP, v0, v1)


def _sample_select_kernel(q_ref, kcmp_ref, vcmp_ref, ocmp_ref, idx_ref, *, t_pos, n_slc, n_lane):
    qr = _query_rows(q_ref[0] * ATTN_SCALE)
    n_cmp = kcmp_ref.shape[1]
    slope = _head_slopes()
    sc = _bdot_nt(qr, kcmp_ref[0])
    c_start = lax.broadcasted_iota(I32, (1, n_cmp), 1) * CMP_STRIDE
    dist = float(t_pos) - (c_start.astype(F32) + (CMP_LEN - 1) / 2)
    vis = c_start + (CMP_LEN - 1) <= t_pos
    sc = jnp.where(vis, sc - slope * dist, NEG)
    m = jnp.max(sc, axis=-1, keepdims=True)
    e = jnp.where(vis, jnp.exp(sc - m), 0.0)
    p = e / jnp.maximum(jnp.sum(e, axis=-1, keepdims=True), 1e-30)
    ocmp_ref[0] = _bdot(p, vcmp_ref[0])

    hrow = lax.broadcasted_iota(I32, (N_HEADS, 1), 0)
    psum = jnp.concatenate(
        [jnp.sum(jnp.where((hrow // GROUP) == g, p, 0.0), axis=0, keepdims=True) for g in range(N_KV_HEADS)]
        + [jnp.zeros((8 - N_KV_HEADS, n_cmp), F32)], axis=0)
    imp = _split_dot(psum, _overlap(n_cmp, n_lane))
    j_row = lax.broadcasted_iota(I32, (1, n_lane), 1)
    cur = t_pos // SEL_BLOCK
    forced = (j_row == 0) | (j_row == cur) | (j_row == cur - 1)
    valid = (j_row * SEL_BLOCK <= t_pos) & (j_row < n_slc)
    imp = jnp.where(valid, jnp.where(forced, FORCED_SCORE, imp), -1.0)
    n_top = min(SEL_TOP, n_slc)
    ii = lax.broadcasted_iota(I32, (n_lane, n_lane), 0)
    jj = lax.broadcasted_iota(I32, (n_lane, n_lane), 1)
    slot = lax.broadcasted_iota(I32, (1, SEL_TOP), 1)
    rows = []
    for g in range(N_KV_HEADS):
        by_lane = jnp.broadcast_to(imp[g:g + 1], (n_lane, n_lane))
        col = jnp.sum(jnp.where(ii == jj, by_lane, 0.0), axis=1, keepdims=True)
        by_row = jnp.broadcast_to(col, (n_lane, n_lane))
        beaten = (by_lane > by_row) | ((by_lane == by_row) & (jj < ii))
        rank_col = jnp.sum(beaten.astype(F32), axis=1, keepdims=True)
        hit = rank_col == slot.astype(F32)
        block = jnp.sum(jnp.where(hit, ii[:, :SEL_TOP].astype(F32), 0.0), axis=0, keepdims=True)
        rows.append(jnp.where(slot < n_top, block, float(n_lane - 1)).astype(I32))
    idx_ref[0] = jnp.concatenate(rows, axis=0)


def _sample_select(q, kcmp, vcmp, t_pos):
    db = q.shape[0]
    n_cmp = kcmp.shape[1]
    n_slc = -(-(t_pos + 1) // SEL_BLOCK)
    n_lane = -(-n_slc // 128) * 128
    return pl.pallas_call(
        functools.partial(_sample_select_kernel, t_pos=t_pos, n_slc=n_slc, n_lane=n_lane),
        grid=(db,),
        in_specs=[pl.BlockSpec((1, 1, ATTN_DIM), lambda i: (i, 0, 0)),
                  pl.BlockSpec((1, n_cmp, KV_DIM), lambda i: (i, 0, 0)),
                  pl.BlockSpec((1, n_cmp, KV_DIM), lambda i: (i, 0, 0))],
        out_specs=[pl.BlockSpec((1, N_HEADS, KV_DIM), lambda i: (i, 0, 0)),
                   pl.BlockSpec((1, N_KV_HEADS, SEL_TOP), lambda i: (i, 0, 0))],
        out_shape=[jax.ShapeDtypeStruct((db, N_HEADS, KV_DIM), F32),
                   jax.ShapeDtypeStruct((db, N_KV_HEADS, SEL_TOP), I32)],
        compiler_params=_params(("arbitrary",)),
        name="sample_select",
    )(q, kcmp, vcmp)


def _sample_attend_kernel(idx_s, pt_s, q_ref, idx_ref, ksn_ref, vsn_ref, kwb_ref, vwb_ref, kwn_ref, vwn_ref,
                          gn_ref, ocmp_ref, kpool, vpool, o_ref, kbuf, vbuf, sem, *, t_pos, n_past_blk):
    b = pl.program_id(0)
    nb = pl.num_programs(0)
    slot = b % 2
    n_sel = N_KV_HEADS * SEL_TOP
    blk_per_page = PAGE_ROWS // SEL_BLOCK

    def block_copy(pool, buf, seq, s, sl, which):
        blk = jnp.minimum(idx_s[seq * n_sel + s], n_past_blk - 1)
        page = pt_s[seq, blk // blk_per_page]
        off = pl.multiple_of((blk % blk_per_page) * SEL_BLOCK, SEL_BLOCK)
        return pltpu.make_async_copy(pool.at[page, pl.ds(off, SEL_BLOCK)], buf.at[sl, s], sem.at[which, sl])

    def fetch(seq, sl):
        for s in range(n_sel):
            block_copy(kpool, kbuf, seq, s, sl, 0).start()
            block_copy(vpool, vbuf, seq, s, sl, 1).start()

    @pl.when(b == 0)
    def _():
        fetch(0, 0)

    @pl.when(b + 1 < nb)
    def _():
        fetch(b + 1, 1 - slot)

    for s in range(n_sel):
        block_copy(kpool, kbuf, b, s, slot, 0).wait()
        block_copy(vpool, vbuf, b, s, slot, 1).wait()

    qr = _query_rows(q_ref[0] * ATTN_SCALE)
    slope = _head_slopes()
    hrow = lax.broadcasted_iota(I32, (N_HEADS, 1), 0)

    def with_new_key(s, mask, v, s_new, new_ok, v_new):
        s = jnp.where(mask, s, NEG)
        s_new = jnp.where(new_ok, s_new, NEG)
        m = jnp.maximum(jnp.max(s, axis=-1, keepdims=True), s_new)
        e = jnp.where(mask, jnp.exp(s - m), 0.0)
        e_new = jnp.where(new_ok, jnp.exp(s_new - m), 0.0)
        l = jnp.sum(e, axis=-1, keepdims=True) + e_new
        return (_bdot(e, v) + e_new * v_new) / jnp.maximum(l, 1e-30)

    n_col = SEL_TOP * SEL_BLOCK
    k_all = kbuf[slot].reshape(n_sel * SEL_BLOCK, KV_DIM)
    v_all = vbuf[slot].reshape(n_sel * SEL_BLOCK, KV_DIM)
    idx8 = jnp.concatenate([idx_ref[0].astype(F32), jnp.zeros((8 - N_KV_HEADS, SEL_TOP), F32)], axis=0)
    expand = (lax.broadcasted_iota(I32, (SEL_TOP, n_col), 0)
              == lax.broadcasted_iota(I32, (SEL_TOP, n_col), 1) // SEL_BLOCK)
    blk8 = _bdot(idx8, expand.astype(BF16))
    blk = jnp.concatenate([blk8[g:g + 1] for g in range(N_KV_HEADS)], axis=1)
    col = lax.broadcasted_iota(I32, (1, N_KV_HEADS * n_col), 1)
    pos = blk * SEL_BLOCK + (col % SEL_BLOCK).astype(F32)
    own = (col // n_col) == (hrow // GROUP)
    mask = own & (blk < n_past_blk)
    s_sel = _bdot_nt(qr, k_all) - slope * (float(t_pos) - pos)
    cur = float(t_pos // SEL_BLOCK)
    has_new = [jnp.max(jnp.where(idx8[g:g + 1] == cur, 1.0, 0.0), axis=-1, keepdims=True)
               for g in range(N_KV_HEADS)]
    new_ok = _per_group(hrow, has_new[0], has_new[1]) > 0.5
    s_new = jnp.sum(qr * ksn_ref[0], axis=-1, keepdims=True)
    o_sel = with_new_key(s_sel, mask, v_all, s_new, new_ok, vsn_ref[0])

    w_buf = kwb_ref.shape[1]
    delta = w_buf - lax.broadcasted_iota(I32, (1, w_buf), 1)
    wmask = (delta <= WINDOW) & (t_pos - delta >= 0)
    s_win = _bdot_nt(qr, kwb_ref[0]) - slope * delta.astype(F32)
    s_wnew = jnp.sum(qr * kwn_ref[0], axis=-1, keepdims=True)
    o_win = with_new_key(s_win, jnp.broadcast_to(wmask, s_win.shape), vwb_ref[0], s_wnew, hrow >= 0, vwn_ref[0])

    lane = lax.broadcasted_iota(I32, (N_HEADS, GATE_PAD), 1)
    gn = jnp.broadcast_to(gn_ref[0], (N_HEADS, GATE_PAD))
    gate = lambda j: jnp.sum(jnp.where(lane == 3 * hrow + j, gn, 0.0), axis=-1, keepdims=True)
    o = gate(0) * ocmp_ref[0] + gate(1) * o_sel + gate(2) * o_win
    o_ref[0] = jnp.concatenate(
        [o[h:h + 1, HEAD_DIM * (h // GROUP):HEAD_DIM * (h // GROUP + 1)] for h in range(N_HEADS)], axis=1)


def _sample_attend(idx, page_table, q, ks_new, vs_new, kw_buf, vw_buf, kw_new, vw_new, gn, ocmp,
                   ks_pool, vs_pool, t_pos):
    db = q.shape[0]
    w_buf = kw_buf.shape[1]
    n_sel = N_KV_HEADS * SEL_TOP
    n_past_blk = page_table.shape[1] * (PAGE_ROWS // SEL_BLOCK)
    row = lambda n: pl.BlockSpec((1, 1, n), lambda i, a, b: (i, 0, 0))
    grid_spec = pltpu.PrefetchScalarGridSpec(
        num_scalar_prefetch=2,
        grid=(db,),
        in_specs=[row(ATTN_DIM),
                  pl.BlockSpec((1, N_KV_HEADS, SEL_TOP), lambda i, a, b: (i, 0, 0)),
                  row(KV_DIM), row(KV_DIM),
                  pl.BlockSpec((1, w_buf, KV_DIM), lambda i, a, b: (i, 0, 0)),
                  pl.BlockSpec((1, w_buf, KV_DIM), lambda i, a, b: (i, 0, 0)),
                  row(KV_DIM), row(KV_DIM), row(GATE_PAD),
                  pl.BlockSpec((1, N_HEADS, KV_DIM), lambda i, a, b: (i, 0, 0)),
                  pl.BlockSpec(memory_space=pl.ANY), pl.BlockSpec(memory_space=pl.ANY)],
        out_specs=row(ATTN_DIM),
        scratch_shapes=[pltpu.VMEM((2, n_sel, SEL_BLOCK, KV_DIM), F32),
                        pltpu.VMEM((2, n_sel, SEL_BLOCK, KV_DIM), F32),
                        pltpu.SemaphoreType.DMA((2, 2))],
    )
    return pl.pallas_call(
        functools.partial(_sample_attend_kernel, t_pos=t_pos, n_past_blk=n_past_blk),
        grid_spec=grid_spec,
        out_shape=jax.ShapeDtypeStruct((db, 1, ATTN_DIM), F32),
        compiler_params=_params(("arbitrary",)),
        name="sample_attend",
    )(idx.reshape(-1), page_table, q, idx, ks_new, vs_new, kw_buf, vw_buf, kw_new, vw_new, gn, ocmp,
      ks_pool, vs_pool)


def _merge_kernel(x_ref, oatt_ref, u_ref, hist_ref, bg_ref, ma_ref, mc_ref, cw_ref,
                  wao_ref, wco_ref, wo_ref, gate1_ref, shift2_ref, scale2_ref, g2_ref,
                  x1_ref, h2_ref, *, per_token_history):
    u = u_ref[0]
    tm = u.shape[0]
    if per_token_history:
        u2, u1 = hist_ref[0], hist_ref[1]
    else:
        prev = hist_ref[0]
        first = pl.program_id(1) == 0
        p1 = jnp.where(first, 0.0, prev[7:8])
        p2 = jnp.where(first, 0.0, prev[6:7])
        r = lax.broadcasted_iota(I32, (tm, 1), 0)
        u1 = jnp.where(r == 0, p1, pltpu.roll(u, 1, 0))
        u2 = jnp.where(r == 0, p2, jnp.where(r == 1, p1, pltpu.roll(u, 2, 0)))
    cw = cw_ref[...]
    conv = u2 * cw[0:1] + u1 * cw[1:2] + u * cw[2:3]
    y_conv = bg_ref[0] * conv
    mix = ma_ref[0] * _bdot(oatt_ref[0], wao_ref[...]) + mc_ref[0] * _bdot(y_conv, wco_ref[...])
    x1 = x_ref[0] + gate1_ref[0] * _bdot(mix, wo_ref[...])
    x1_ref[0] = x1
    h2_ref[0] = _rms_mod(x1, g2_ref[...], shift2_ref[0], scale2_ref[0])


def _merge(x, oatt, u, hist, bg, ma, mc, conv_w, wao, wco, wo, gate1, shift2, scale2, g2, tm):
    b, s, d = x.shape
    tm = min(tm, s)
    per_token = hist is not None
    r = gate1.shape[1]
    rb = 1 if r == 1 else tm
    mod_map = (lambda i, j: (i, 0, 0)) if r == 1 else (lambda i, j: (i, j, 0))
    row = lambda n: pl.BlockSpec((1, tm, n), lambda i, j: (i, j, 0))
    const = lambda a: pl.BlockSpec(a.shape, lambda i, j: (0, 0))
    mod = pl.BlockSpec((1, rb, d), mod_map)
    if per_token:
        assert b == 1
        hist_arr = hist
        hist_spec = pl.BlockSpec((2, tm, CONV_DIM), lambda i, j: (0, j, 0))
    else:
        hist_arr = u
        hist_spec = pl.BlockSpec((1, 8, CONV_DIM), lambda i, j: (i, jnp.maximum(j * (tm // 8) - 1, 0), 0))
    return pl.pallas_call(
        functools.partial(_merge_kernel, per_token_history=per_token),
        grid=(b, s // tm),
        in_specs=[row(d), row(ATTN_DIM), row(CONV_DIM), hist_spec, row(CONV_DIM), row(d), row(d),
                  const(conv_w), const(wao), const(wco), const(wo), mod, mod, mod, const(g2)],
        out_specs=[row(d), row(d)],
        out_shape=[jax.ShapeDtypeStruct((b, s, d), F32)] * 2,
        compiler_params=_params(("arbitrary", "arbitrary")),
        name="merge",
    )(x, oatt, u, hist_arr, bg, ma, mc, conv_w, wao, wco, wo, gate1, shift2, scale2, g2)


def _route_kernel(h_ref, rwt_ref, rb_ref, e_ref, w_ref, pos_ref, cnt_ref):
    @pl.when(pl.program_id(0) == 0)
    def _():
        cnt_ref[...] = jnp.zeros_like(cnt_ref)

    tm = h_ref.shape[0]
    per_group = N_EXPERTS // N_GROUPS
    aff = _sigmoid(_bdot_nt(rwt_ref[...], h_ref[...]))
    biased = aff + rb_ref[...]
    row = lax.broadcasted_iota(I32, (N_EXPERTS, tm), 0)
    lrow = lax.broadcasted_iota(I32, (per_group, tm), 0)
    ninf = -jnp.inf

    score = []
    for g in range(N_GROUPS):
        xg = biased[g * per_group:(g + 1) * per_group]
        m1 = jnp.max(xg, axis=0, keepdims=True)
        i1 = jnp.min(jnp.where(xg == m1, lrow, per_group), axis=0, keepdims=True)
        m2 = jnp.max(jnp.where(lrow == i1, ninf, xg), axis=0, keepdims=True)
        score.append(m1 + m2)
    cand = []
    for a in range(N_GROUPS):
        ahead = jnp.zeros((1, tm), F32)
        for b in range(N_GROUPS):
            if b != a:
                wins = (score[b] > score[a]) | ((score[b] == score[a]) & (b < a))
                ahead = ahead + wins.astype(F32)
        cand.append(jnp.where(ahead < TOPK_GROUPS, biased[a * per_group:(a + 1) * per_group], ninf))
    cand = jnp.concatenate(cand, axis=0)

    chosen = jnp.zeros((N_EXPERTS, tm), F32)
    e_rows, w_rows = [], []
    for _ in range(TOP_K):
        m = jnp.max(cand, axis=0, keepdims=True)
        idx = jnp.min(jnp.where(cand == m, row, N_EXPERTS), axis=0, keepdims=True)
        hit = row == idx
        e_rows.append(idx)
        w_rows.append(jnp.sum(jnp.where(hit, aff, 0.0), axis=0, keepdims=True))
        cand = jnp.where(hit, ninf, cand)
        chosen = chosen + hit.astype(F32)
    total = w_rows[0]
    for w in w_rows[1:]:
        total = total + w
    w_rows = [w / total * ROUTED_SCALE for w in w_rows]

    earlier = (lax.broadcasted_iota(I32, (tm, tm), 0) < lax.broadcasted_iota(I32, (tm, tm), 1)).astype(BF16)
    before = _bdot(chosen, earlier) + cnt_ref[...]
    pos_rows = [jnp.sum(jnp.where(row == e, before, 0.0), axis=0, keepdims=True) for e in e_rows]
    cnt_ref[...] += jnp.sum(chosen, axis=1, keepdims=True)

    e_ref[...] = jnp.concatenate(e_rows, axis=0)
    w_ref[...] = jnp.concatenate(w_rows, axis=0)
    pos_ref[...] = jnp.concatenate(pos_rows, axis=0).astype(I32)


def _route(h, rwt, rb, tm):
    n, d = h.shape
    tm = min(tm, n)
    slot = pl.BlockSpec((TOP_K, tm), lambda i: (0, i))
    return pl.pallas_call(
        _route_kernel,
        grid=(n // tm,),
        in_specs=[pl.BlockSpec((tm, d), lambda i: (i, 0)),
                  pl.BlockSpec((N_EXPERTS, d), lambda i: (0, 0)),
                  pl.BlockSpec((N_EXPERTS, 1), lambda i: (0, 0))],
        out_specs=[slot, slot, slot, pl.BlockSpec((N_EXPERTS, 1), lambda i: (0, 0))],
        out_shape=[jax.ShapeDtypeStruct((TOP_K, n), I32), jax.ShapeDtypeStruct((TOP_K, n), F32),
                   jax.ShapeDtypeStruct((TOP_K, n), I32), jax.ShapeDtypeStruct((N_EXPERTS, 1), F32)],
        compiler_params=_params(("arbitrary",)),
        name="moe_route",
    )(h, rwt, rb)


def _dispatch_kernel(dest_ref, h_ref, xs_in, xs_out, sem):
    del xs_in
    tm = h_ref.shape[0]

    def row_copy(n, d):
        return pltpu.make_async_copy(h_ref.at[pl.ds(n, 1)], xs_out.at[pl.ds(d, 1)], sem)

    def issue(n, c):
        for k in range(TOP_K):
            row_copy(n, dest_ref[n * TOP_K + k]).start()
        return c

    def drain(n, c):
        for k in range(TOP_K):
            row_copy(0, 0).wait()
        return c

    lax.fori_loop(0, tm, issue, 0)
    lax.fori_loop(0, tm, drain, 0)


def _dispatch(dest_flat, h, n_rows, tm):
    n, d = h.shape
    tm = min(tm, n)
    xs = jnp.zeros((n_rows, d), F32)
    return pl.pallas_call(
        _dispatch_kernel,
        grid=(n // tm,),
        in_specs=[pl.BlockSpec((tm * TOP_K,), lambda i: (i,), memory_space=pltpu.SMEM),
                  pl.BlockSpec((tm, d), lambda i: (i, 0)),
                  pl.BlockSpec(memory_space=pl.ANY)],
        out_specs=pl.BlockSpec(memory_space=pl.ANY),
        out_shape=jax.ShapeDtypeStruct((n_rows, d), F32),
        scratch_shapes=[pltpu.SemaphoreType.DMA(())],
        input_output_aliases={2: 0},
        compiler_params=_params(("arbitrary",)),
        name="moe_dispatch",
    )(dest_flat, h, xs)


def _expert_kernel(be_ref, used_ref, x_ref, wg_ref, wu_ref, wd_ref, o_ref):
    i = pl.program_id(0)

    @pl.when(i < used_ref[0])
    def _():
        x = x_ref[...].astype(BF16)
        g = jnp.dot(x, wg_ref[0], preferred_element_type=F32)
        u = jnp.dot(x, wu_ref[0], preferred_element_type=F32)
        o_ref[...] = jnp.dot((_silu(g) * u).astype(BF16), wd_ref[0], preferred_element_type=F32)

    @pl.when(i >= used_ref[0])
    def _():
        o_ref[...] = jnp.zeros_like(o_ref)


def _experts(block_e, n_used, xs, wg, wu, wd):
    n_rows, d = xs.shape
    ff = wg.shape[2]
    grid_spec = pltpu.PrefetchScalarGridSpec(
        num_scalar_prefetch=2,
        grid=(n_rows // MOE_ROWS,),
        in_specs=[pl.BlockSpec((MOE_ROWS, d), lambda i, be, nu: (i, 0)),
                  pl.BlockSpec((1, d, ff), lambda i, be, nu: (be[i], 0, 0)),
                  pl.BlockSpec((1, d, ff), lambda i, be, nu: (be[i], 0, 0)),
                  pl.BlockSpec((1, ff, d), lambda i, be, nu: (be[i], 0, 0))],
        out_specs=pl.BlockSpec((MOE_ROWS, d), lambda i, be, nu: (i, 0)),
    )
    return pl.pallas_call(
        _expert_kernel,
        grid_spec=grid_spec,
        out_shape=jax.ShapeDtypeStruct((n_rows, d), F32),
        compiler_params=_params(("arbitrary",)),
        name="moe_experts",
    )(block_e, n_used, xs, wg, wu, wd)


def _combine_kernel(dest_ref, w_ref, x1_ref, h2_ref, gate2_ref, sg_ref, su_ref, sd_ref, ys_hbm,
                    o_ref, gbuf, sem):
    tm = x1_ref.shape[0]

    def row_copy(n, k, d):
        return pltpu.make_async_copy(ys_hbm.at[pl.ds(d, 1)], gbuf.at[k, pl.ds(n, 1)], sem)

    def issue(n, c):
        for k in range(TOP_K):
            row_copy(n, k, dest_ref[n * TOP_K + k]).start()
        return c

    def drain(n, c):
        for k in range(TOP_K):
            row_copy(0, 0, 0).wait()
        return c

    lax.fori_loop(0, tm, issue, 0)
    h = h2_ref[...].astype(BF16)
    g = jnp.dot(h, sg_ref[...], preferred_element_type=F32)
    u = jnp.dot(h, su_ref[...], preferred_element_type=F32)
    y = jnp.dot((_silu(g) * u).astype(BF16), sd_ref[...], preferred_element_type=F32)
    lax.fori_loop(0, tm, drain, 0)
    w = w_ref[...]
    for k in range(TOP_K):
        y = y + w[:, k:k + 1] * gbuf[k]
    o_ref[...] = x1_ref[...] + gate2_ref[0] * y


def _combine(dest_flat, w, x1, h2, gate2, tokens_per_gate, sg, su, sd, ys, tm):
    n, d = x1.shape
    tm = min(tm, n)
    row = pl.BlockSpec((tm, d), lambda i: (i, 0))
    const = lambda a: pl.BlockSpec(a.shape, lambda i: (0, 0))
    if gate2.shape[1] == 1:
        gate_spec = pl.BlockSpec((1, 1, d), lambda i: ((i * tm) // tokens_per_gate, 0, 0))
    else:
        gate_spec = pl.BlockSpec((1, tm, d), lambda i: (0, i, 0))
    return pl.pallas_call(
        _combine_kernel,
        grid=(n // tm,),
        in_specs=[pl.BlockSpec((tm * TOP_K,), lambda i: (i,), memory_space=pltpu.SMEM),
                  pl.BlockSpec((tm, TOP_K), lambda i: (i, 0)),
                  row, row, gate_spec, const(sg), const(su), const(sd),
                  pl.BlockSpec(memory_space=pl.ANY)],
        out_specs=row,
        out_shape=jax.ShapeDtypeStruct((n, d), F32),
        scratch_shapes=[pltpu.VMEM((TOP_K, tm, d), F32), pltpu.SemaphoreType.DMA(())],
        compiler_params=_params(("arbitrary",)),
        name="moe_combine",
    )(dest_flat, w, x1, h2, gate2, sg, su, sd, ys)


def _moe(h2, x1, gate2, tokens_per_gate, mw):
    n = h2.shape[0]
    e_t, w_t, pos_t, counts = _route(h2, mw["rwt"], mw["rb"], 256)
    counts = counts.reshape(-1).astype(I32)
    padded = (counts + MOE_ROWS - 1) // MOE_ROWS * MOE_ROWS
    pad_end = jnp.cumsum(padded)
    n_blocks = n * TOP_K // MOE_ROWS + N_EXPERTS
    dest = (jnp.take(pad_end - padded, e_t) + pos_t).T.reshape(-1)
    block_e = jnp.minimum(jnp.searchsorted(pad_end, jnp.arange(n_blocks, dtype=I32) * MOE_ROWS, side="right"),
                          N_EXPERTS - 1).astype(I32)
    n_used = (pad_end[-1:] // MOE_ROWS).astype(I32)
    xs = _dispatch(dest, h2, n_blocks * MOE_ROWS, 256)
    ys = _experts(block_e, n_used, xs, mw["wg"], mw["wu"], mw["wd"])
    return _combine(dest, w_t.T, x1, h2, gate2, tokens_per_gate, mw["sg"], mw["su"], mw["sd"], ys, 128)


def _pack_w_in(w_in):
    n_gate = N_HEADS * 3
    g0 = ATTN_DIM + 6 * KV_DIM
    gate = jnp.pad(w_in[:, g0:g0 + n_gate], ((0, 0), (0, GATE_PAD - n_gate)))
    return jnp.concatenate([w_in[:, :g0], gate, w_in[:, g0 + n_gate:]], axis=1).astype(BF16)


def _compress_weights(cmp_pe, cmp_w1, cmp_w2, g_kcmp):
    eye = jnp.eye(N_KV_HEADS, dtype=F32)
    sub = CMP_LEN // CMP_STRIDE

    def w1_cat(w1):
        w = w1.reshape(sub, CMP_STRIDE, HEAD_DIM, CMP_HID)
        mats = [jnp.einsum("sdh,gk->sgdkh", w[j], eye).reshape(CMP_STRIDE * KV_DIM, N_KV_HEADS * CMP_HID)
                for j in range(sub)]
        return jnp.concatenate(mats, axis=1).astype(BF16)

    def w2_blk(w2):
        return jnp.einsum("hd,gk->ghkd", w2, eye).reshape(N_KV_HEADS * CMP_HID, KV_DIM).astype(BF16)

    def pe_rows(pe):
        p = pe.reshape(sub, CMP_STRIDE, 1, HEAD_DIM)
        p = jnp.broadcast_to(p, (sub, CMP_STRIDE, N_KV_HEADS, HEAD_DIM)).reshape(sub, 1, CMP_STRIDE * KV_DIM)
        return jnp.broadcast_to(p, (sub, 8, CMP_STRIDE * KV_DIM))

    assert sub == 2
    return {
        "w1k": w1_cat(cmp_w1[0]), "w1v": w1_cat(cmp_w1[1]),
        "w2k": w2_blk(cmp_w2[0]), "w2v": w2_blk(cmp_w2[1]),
        "pe": jnp.concatenate([pe_rows(cmp_pe[0]), pe_rows(cmp_pe[1])], axis=0),
        "gk": jnp.tile(g_kcmp, N_KV_HEADS).reshape(1, KV_DIM),
    }


def _layer(l, x_p, x_s, c_all, caches, page_table, w):
    ckc, cvc, cks, cvs, ckw, cvw, sconv = caches
    b, s, d = x_p.shape
    db, ds = x_s.shape[:2]
    assert ds == 1, "one new token per sample sequence"
    n_pool = ckc.shape[1]
    t_pos = page_table.shape[1] * PAGE_ROWS
    hist = CONV_WIDTH - 1

    ada = _ada(c_all, w["w_ada"][l], w["b_ada"][l])
    mods = [ada[:, k * d:(k + 1) * d] for k in range(6)]
    mod_p = [m[:b, None] for m in mods]
    mod_s = [m[b:][None] for m in mods]

    qkg = w["qk_norm_g"][l]
    gq = jnp.tile(qkg[0], N_HEADS).reshape(1, -1)
    gks = jnp.tile(qkg[2], N_KV_HEADS).reshape(1, -1)
    gkw = jnp.tile(qkg[3], N_KV_HEADS).reshape(1, -1)
    g1 = w["norm1_g"][l].reshape(1, d)
    g2 = w["norm2_g"][l].reshape(1, d)
    w_packed = _pack_w_in(w["w_in"][l])
    cw = _compress_weights(w["cmp_pe"][l], w["cmp_w1"][l], w["cmp_w2"][l], qkg[1])
    wao, wco, wo = (w[k][l].astype(BF16) for k in ("w_attn_out", "w_conv_out", "w_o"))
    mw = {"rwt": w["router_w"][l].T.astype(BF16), "rb": w["router_b"][l].reshape(-1, 1),
          "wg": w["exp_w_gate"][l].astype(BF16), "wu": w["exp_w_up"][l].astype(BF16),
          "wd": w["exp_w_down"][l].astype(BF16), "sg": w["shared_w_gate"][l].astype(BF16),
          "su": w["shared_w_up"][l].astype(BF16), "sd": w["shared_w_down"][l].astype(BF16)}
    pages = lambda a: a.reshape(-1, PAGE_ROWS, KV_DIM)

    (q, kc, vc, ks, vs, kw, vw, gn, u, bg, ma, mc) = _in_proj(
        x_p, mod_p[0], mod_p[1], g1, w_packed, gq, gks, gkw, 256)
    own_pages = jnp.arange(b * s // PAGE_ROWS, dtype=I32).reshape(b, s // PAGE_ROWS)
    kcmp, vcmp = _compress(own_pages, pages(kc), pages(vc), cw)
    o_att = _nsa_prompt(q, kcmp, vcmp, ks, vs, kw, vw, gn, 128)
    x1, h2 = _merge(x_p, o_att, u, None, bg, ma, mc, w["conv_w"][l], wao, wco, wo,
                    mod_p[2], mod_p[3], mod_p[4], g2, 256)
    y_p = _moe(h2.reshape(b * s, d), x1.reshape(b * s, d), mod_p[5], s, mw).reshape(b, s, d)
    keep = min(WINDOW, s)
    heads = lambda a: a.reshape(a.shape[0], a.shape[1], N_KV_HEADS, HEAD_DIM)
    p_state = (heads(kc), heads(vc), heads(ks), heads(vs), heads(kw[:, -keep:]), heads(vw[:, -keep:]),
               u[:, -hist:])

    (q, kc, vc, ks, vs, kw, vw, gn, u, bg, ma, mc) = _in_proj(
        x_s.reshape(1, db, d), mod_s[0], mod_s[1], g1, w_packed, gq, gks, gkw, 128)
    per_seq = lambda a: a.reshape(db, 1, a.shape[-1])
    kcmp, vcmp = _compress(page_table, pages(ckc[l]), pages(cvc[l]), cw)
    o_cmp, idx = _sample_select(per_seq(q), kcmp, vcmp, t_pos)
    w_buf = ckw.shape[2]
    o_att = _sample_attend(idx, page_table, per_seq(q), per_seq(ks), per_seq(vs),
                           ckw[l].reshape(db, w_buf, KV_DIM), cvw[l].reshape(db, w_buf, KV_DIM),
                           per_seq(kw), per_seq(vw), per_seq(gn), o_cmp, pages(cks[l]), pages(cvs[l]), t_pos)
    x1, h2 = _merge(x_s.reshape(1, db, d), o_att.reshape(1, db, ATTN_DIM), u, jnp.swapaxes(sconv[l], 0, 1),
                    bg, ma, mc, w["conv_w"][l], wao, wco, wo, mod_s[2], mod_s[3], mod_s[4], g2, 128)
    y_s = _moe(h2.reshape(db, d), x1.reshape(db, d), mod_s[5], 1, mw).reshape(db, 1, d)
    new_row = lambda a: a.reshape(db, 1, N_KV_HEADS, HEAD_DIM)
    s_state = (new_row(kc), new_row(vc), new_row(ks), new_row(vs),
               jnp.concatenate([ckw[l], new_row(kw)], axis=1)[:, -w_buf:],
               jnp.concatenate([cvw[l], new_row(vw)], axis=1)[:, -w_buf:],
               jnp.concatenate([sconv[l], u.reshape(db, 1, CONV_DIM)], axis=1)[:, -hist:])
    return y_p, y_s, p_state + s_state


def kernel(x_prompt, x_sample, cache_k_cmp, cache_v_cmp, cache_k_sel, cache_v_sel, cache_k_win, cache_v_win,
           state_conv, page_table, c_prompt, c_sample, w_ada, b_ada, norm1_g, norm2_g, w_in, qk_norm_g,
           cmp_pe, cmp_w1, cmp_w2, conv_w, w_attn_out, w_conv_out, w_o, router_w, router_b,
           exp_w_gate, exp_w_up, exp_w_down, shared_w_gate, shared_w_up, shared_w_down):
    w = dict(w_ada=w_ada, b_ada=b_ada, norm1_g=norm1_g, norm2_g=norm2_g, w_in=w_in, qk_norm_g=qk_norm_g,
             cmp_pe=cmp_pe, cmp_w1=cmp_w1, cmp_w2=cmp_w2, conv_w=conv_w, w_attn_out=w_attn_out,
             w_conv_out=w_conv_out, w_o=w_o, router_w=router_w, router_b=router_b, exp_w_gate=exp_w_gate,
             exp_w_up=exp_w_up, exp_w_down=exp_w_down, shared_w_gate=shared_w_gate, shared_w_up=shared_w_up,
             shared_w_down=shared_w_down)
    caches = (cache_k_cmp, cache_v_cmp, cache_k_sel, cache_v_sel, cache_k_win, cache_v_win, state_conv)
    c_all = jnp.concatenate([c_prompt, c_sample], axis=0)
    x_p, x_s = x_prompt, x_sample
    states = []
    for l in range(w_ada.shape[0]):
        x_p, x_s, st = _layer(l, x_p, x_s, c_all, caches, page_table.astype(I32), w)
        states.append(st)
    return (x_p, x_s) + tuple(jnp.stack(s) for s in zip(*states))
```

```python
import functools

import jax
import jax.numpy as jnp
from jax import lax
from jax.experimental import pallas as pl
from jax.experimental.pallas import tpu as pltpu

F32 = jnp.float32
BF16 = jnp.bfloat16
I32 = jnp.int32

D_MODEL = 1024
N_HEADS = 8
HEAD_DIM = 64
N_KV_HEADS = 2
GROUP = N_HEADS // N_KV_HEADS
ATTN_DIM = N_HEADS * HEAD_DIM
KV_DIM = N_KV_HEADS * HEAD_DIM
ATTN_SCALE = HEAD_DIM ** -0.5
CMP_LEN = 32
CMP_STRIDE = 16
CMP_HID = 4 * HEAD_DIM
SEL_BLOCK = 64
SEL_TOP = 16
FORCED_SCORE = 1e6
WINDOW = 512
CONV_DIM = D_MODEL // 2
CONV_WIDTH = 3
N_EXPERTS = 256
TOP_K = 8
N_GROUPS = 8
TOPK_GROUPS = 4
EXPERT_FF = D_MODEL // 4
ROUTED_SCALE = 2.5
NORM_EPS = 1e-6
PAGE_ROWS = 128
CHUNKS_PER_PAGE = PAGE_ROWS // CMP_STRIDE
NEG = -1e30
GATE_PAD = 128
MOE_ROWS = 256
VMEM_LIMIT = 56 * 1024 * 1024


def _params(sem):
    return pltpu.CompilerParams(dimension_semantics=sem, vmem_limit_bytes=VMEM_LIMIT)


def _bdot(a, b):
    return jnp.dot(a.astype(BF16), b.astype(BF16), preferred_element_type=F32)


def _bdot_nt(a, b):
    return lax.dot_general(a.astype(BF16), b.astype(BF16), (((1,), (1,)), ((), ())),
                           preferred_element_type=F32)


def _split_dot(a, b_bf16, nt=False):
    hi = a.astype(BF16)
    lo = (a - hi.astype(F32)).astype(BF16)
    f = _bdot_nt if nt else _bdot
    return f(hi, b_bf16) + f(lo, b_bf16)


def _sigmoid(x):
    return 1.0 / (1.0 + jnp.exp(-x))


def _silu(x):
    return x * _sigmoid(x)


def _gelu_tanh(x):
    return 0.5 * x * (1.0 + jnp.tanh(0.7978845608028654 * (x + 0.044715 * (x * x * x))))


def _head_rms(z, gain):
    r = lax.broadcasted_iota(I32, (128, 128), 0) // HEAD_DIM
    c = lax.broadcasted_iota(I32, (128, 128), 1) // HEAD_DIM
    seg = (r == c).astype(BF16)
    x2 = z * z
    parts = [_split_dot(x2[:, j:j + 128], seg) for j in range(0, z.shape[1], 128)]
    ss = parts[0] if len(parts) == 1 else jnp.concatenate(parts, axis=1)
    return z * lax.rsqrt(ss * (1.0 / HEAD_DIM) + NORM_EPS) * gain


def _rms_mod(x, g, shift, scale):
    y = x * lax.rsqrt(jnp.mean(x * x, axis=-1, keepdims=True) + NORM_EPS) * g
    return y * (1.0 + scale) + shift


def _ada_kernel(c_ref, w_ref, b_ref, o_ref):
    o_ref[...] = _bdot(_silu(c_ref[...]), w_ref[...]) + b_ref[...]


def _ada(c, w, b):
    m, d = c.shape
    n = w.shape[1]
    tn = 1024
    return pl.pallas_call(
        _ada_kernel,
        grid=(n // tn,),
        in_specs=[pl.BlockSpec((m, d), lambda j: (0, 0)),
                  pl.BlockSpec((d, tn), lambda j: (0, j)),
                  pl.BlockSpec((1, tn), lambda j: (0, j))],
        out_specs=pl.BlockSpec((m, tn), lambda j: (0, j)),
        out_shape=jax.ShapeDtypeStruct((m, n), F32),
        compiler_params=_params(("arbitrary",)),
        name="ada_ln",
    )(c, w, b.reshape(1, n))


_C_KV = 0
_C_X = 6 * KV_DIM
_C_B = _C_X + CONV_DIM
_C_C = _C_B + CONV_DIM
_C_MA = _C_C + CONV_DIM
_C_MC = _C_MA + D_MODEL
_R_Q = 0
_R_VS = ATTN_DIM
_R_VW = _R_VS + KV_DIM
_R_G = _R_VW + KV_DIM
GATE_ROWS = 32
_R_END = _R_G + GATE_ROWS
TOK_TILE = 128


def _inproj_kernel(x_ref, shift_ref, scale_ref, g1_ref, w_ref, wt_ref, gq_ref, gks_ref, gkw_ref,
                   qt_ref, kc_ref, vc_ref, ks_ref, vs_ref, kw_ref, vw_ref, ksb_ref, kwb_ref,
                   vst_ref, vwt_ref, gnt_ref, u_ref, bg_ref, ma_ref, mc_ref):
    h = _rms_mod(x_ref[0], g1_ref[...], shift_ref[0], scale_ref[0]).astype(BF16)

    def proj(c0, n):
        return jnp.dot(h, w_ref[:, c0:c0 + n], preferred_element_type=F32)

    kc_ref[0] = proj(_C_KV, KV_DIM)
    vc_ref[0] = proj(_C_KV + KV_DIM, KV_DIM)
    ks = _head_rms(proj(_C_KV + 2 * KV_DIM, KV_DIM), gks_ref[...])
    ks_ref[0] = ks
    ksb_ref[0] = ks.astype(BF16)
    vs_ref[0] = proj(_C_KV + 3 * KV_DIM, KV_DIM)
    kw = _head_rms(proj(_C_KV + 4 * KV_DIM, KV_DIM), gkw_ref[...])
    kw_ref[0] = kw
    kwb_ref[0] = kw.astype(BF16)
    vw_ref[0] = proj(_C_KV + 5 * KV_DIM, KV_DIM)
    u_ref[0] = proj(_C_C, CONV_DIM) * proj(_C_X, CONV_DIM)
    bg_ref[0] = proj(_C_B, CONV_DIM)
    ma_ref[0] = _sigmoid(proj(_C_MA, D_MODEL))
    mc_ref[0] = _sigmoid(proj(_C_MC, D_MODEL))

    gq = gq_ref[...]
    for r in range(h.shape[0] // TOK_TILE):
        zt = _bdot_nt(wt_ref[...], h[r * TOK_TILE:(r + 1) * TOK_TILE])
        heads = []
        for hd in range(N_HEADS):
            z = zt[HEAD_DIM * hd:HEAD_DIM * (hd + 1)]
            ms = jnp.sum(z * z, axis=0, keepdims=True) * (1.0 / HEAD_DIM)
            heads.append(z * lax.rsqrt(ms + NORM_EPS) * gq[HEAD_DIM * hd:HEAD_DIM * (hd + 1)] * ATTN_SCALE)
        qt_ref[0, r] = jnp.concatenate(heads, axis=0).astype(BF16)
        vst_ref[0, r] = zt[_R_VS:_R_VS + KV_DIM].astype(BF16)
        vwt_ref[0, r] = zt[_R_VW:_R_VW + KV_DIM].astype(BF16)
        gnt_ref[0, r] = _sigmoid(zt[_R_G:_R_G + GATE_ROWS])


def _in_proj(x, shift, scale, g1, w_rows, w_feat, gq, gks, gkw, tm):
    b, s, d = x.shape
    tm = min(tm, s)
    assert s % tm == 0 and tm % TOK_TILE == 0
    r = shift.shape[1]
    rb = 1 if r == 1 else tm
    mod_map = (lambda i, j: (i, 0, 0)) if r == 1 else (lambda i, j: (i, j, 0))
    row = lambda n: pl.BlockSpec((1, tm, n), lambda i, j: (i, j, 0))
    feat = lambda n: pl.BlockSpec((1, tm // TOK_TILE, n, TOK_TILE), lambda i, j: (i, j, 0, 0))
    const = lambda a: pl.BlockSpec(a.shape, lambda i, j: (0, 0))
    rows_f32 = lambda n: jax.ShapeDtypeStruct((b, s, n), F32)
    rows_b16 = lambda n: jax.ShapeDtypeStruct((b, s, n), BF16)
    feat_sh = lambda n, dt: jax.ShapeDtypeStruct((b, s // TOK_TILE, n, TOK_TILE), dt)
    out_specs = ([feat(ATTN_DIM)] + [row(KV_DIM)] * 8 + [feat(KV_DIM), feat(KV_DIM), feat(GATE_ROWS)]
                 + [row(CONV_DIM), row(CONV_DIM), row(d), row(d)])
    out_shape = ([feat_sh(ATTN_DIM, BF16)] + [rows_f32(KV_DIM)] * 6 + [rows_b16(KV_DIM)] * 2
                 + [feat_sh(KV_DIM, BF16), feat_sh(KV_DIM, BF16), feat_sh(GATE_ROWS, F32)]
                 + [rows_f32(CONV_DIM), rows_f32(CONV_DIM), rows_f32(d), rows_f32(d)])
    return pl.pallas_call(
        _inproj_kernel,
        grid=(b, s // tm),
        in_specs=[row(d), pl.BlockSpec((1, rb, d), mod_map), pl.BlockSpec((1, rb, d), mod_map),
                  const(g1), const(w_rows), const(w_feat), const(gq), const(gks), const(gkw)],
        out_specs=out_specs,
        out_shape=out_shape,
        compiler_params=_params(("arbitrary", "arbitrary")),
        name="in_proj",
    )(x, shift, scale, g1, w_rows, w_feat, gq, gks, gkw)


def _compress_kernel(pt_ref, kpool, vpool, w1k_ref, w1v_ref, pe_ref, w2k_ref, w2v_ref, gk_ref,
                     kcmp_ref, vcmp_ref, kbuf, vbuf, sem, *, n_pages):
    b = pl.program_id(0)
    nb = pl.num_programs(0)
    slot = b % 2

    def page_copy(pool, buf, seq, p, sl, which):
        return pltpu.make_async_copy(pool.at[pt_ref[seq, p]], buf.at[sl, p], sem.at[which, sl])

    def fetch(seq, sl):
        for p in range(n_pages):
            page_copy(kpool, kbuf, seq, p, sl, 0).start()
            page_copy(vpool, vbuf, seq, p, sl, 1).start()

    @pl.when(b == 0)
    def _():
        fetch(0, 0)

    @pl.when(b + 1 < nb)
    def _():
        fetch(b + 1, 1 - slot)

    for p in range(n_pages):
        page_copy(kpool, kbuf, b, p, slot, 0).wait()
        page_copy(vpool, vbuf, b, p, slot, 1).wait()

    n_chunk = n_pages * CHUNKS_PER_PAGE
    half = N_KV_HEADS * CMP_HID

    def summarise(buf, w1_ref, w2_ref, pe0, pe1):
        x = buf[slot].reshape(n_chunk, buf.shape[-1]).astype(BF16)
        a = jnp.dot(x, w1_ref[...], preferred_element_type=F32)
        pe_term = (jnp.dot(pe0.astype(BF16), w1_ref[:, :half], preferred_element_type=F32)
                   + jnp.dot(pe1.astype(BF16), w1_ref[:, half:], preferred_element_type=F32))
        nxt = pltpu.roll(a[:, half:], n_chunk - 1, 0)
        hid = a[:, :half] + nxt + pe_term[0:1]
        return jnp.dot(_gelu_tanh(hid).astype(BF16), w2_ref[...], preferred_element_type=F32)

    kcmp_ref[0] = _head_rms(summarise(kbuf, w1k_ref, w2k_ref, pe_ref[0], pe_ref[1]), gk_ref[...])
    vcmp_ref[0] = summarise(vbuf, w1v_ref, w2v_ref, pe_ref[2], pe_ref[3])


def _compress(page_table, kpool, vpool, cw):
    n_seq, n_pages = page_table.shape
    n_chunk = n_pages * CHUNKS_PER_PAGE
    width = CMP_STRIDE * KV_DIM
    kp = kpool.reshape(-1, CHUNKS_PER_PAGE, width)
    vp = vpool.reshape(-1, CHUNKS_PER_PAGE, width)
    const = lambda a: pl.BlockSpec(a.shape, lambda i, pt: (0,) * a.ndim)
    out = pl.BlockSpec((1, n_chunk, KV_DIM), lambda i, pt: (i, 0, 0))
    grid_spec = pltpu.PrefetchScalarGridSpec(
        num_scalar_prefetch=1,
        grid=(n_seq,),
        in_specs=[pl.BlockSpec(memory_space=pl.ANY), pl.BlockSpec(memory_space=pl.ANY),
                  const(cw["w1k"]), const(cw["w1v"]), const(cw["pe"]),
                  const(cw["w2k"]), const(cw["w2v"]), const(cw["gk"])],
        out_specs=[out, out],
        scratch_shapes=[pltpu.VMEM((2, n_pages, CHUNKS_PER_PAGE, width), F32),
                        pltpu.VMEM((2, n_pages, CHUNKS_PER_PAGE, width), F32),
                        pltpu.SemaphoreType.DMA((2, 2))],
    )
    return pl.pallas_call(
        functools.partial(_compress_kernel, n_pages=n_pages),
        grid_spec=grid_spec,
        out_shape=[jax.ShapeDtypeStruct((n_seq, n_chunk, KV_DIM), F32)] * 2,
        compiler_params=_params(("arbitrary",)),
        name="compress",
    )(page_table, kp, vp, cw["w1k"], cw["w1v"], cw["pe"], cw["w2k"], cw["w2v"], cw["gk"])


def _overlap(n_cmp, n_slc):
    c = lax.broadcasted_iota(I32, (n_cmp, n_slc), 0) * CMP_STRIDE
    j = lax.broadcasted_iota(I32, (n_cmp, n_slc), 1) * SEL_BLOCK
    return ((c < j + SEL_BLOCK) & (c + CMP_LEN > j)).astype(BF16)


def _overlap_t(n_slc, n_cmp):
    j = lax.broadcasted_iota(I32, (n_slc, n_cmp), 0) * SEL_BLOCK
    c = lax.broadcasted_iota(I32, (n_slc, n_cmp), 1) * CMP_STRIDE
    return ((c < j + SEL_BLOCK) & (c + CMP_LEN > j)).astype(BF16)


def _nsa_prompt_kernel(qt_ref, gnt_ref, kcmp_ref, vcmp_ref, ksb_ref, vst_ref, kwb_ref, vwt_ref, o_ref,
                       qp_ref, selb_ref, rowb_ref, acc_ref, ot_ref, *, tq, n_slc):
    tk = TOK_TILE
    i = pl.program_id(1)
    t0 = i * tq
    n_cmp = kcmp_ref.shape[1]
    n_top = min(SEL_TOP, n_slc)
    width = N_HEADS * tq
    gw = GROUP * tq
    slopes = [2.0 ** -(h + 1) for h in range(N_HEADS)]
    slope_row = jnp.concatenate([jnp.full((1, tq), s, F32) for s in slopes], axis=1)
    lane_t = t0 + lax.broadcasted_iota(I32, (1, tq), 1)
    gnt = jnp.concatenate([gnt_ref[0, r] for r in range(tq // TOK_TILE)], axis=1)
    gate = lambda h, j: gnt[3 * h + j:3 * h + j + 1]
    head = lambda a, h: a[:, h * tq:(h + 1) * tq]

    zero = jnp.zeros((HEAD_DIM, TOK_TILE), BF16)
    for h in range(N_HEADS):
        for r in range(tq // TOK_TILE):
            qh = qt_ref[0, r, HEAD_DIM * h:HEAD_DIM * (h + 1), :]
            c0 = h * tq + r * TOK_TILE
            qp_ref[:, c0:c0 + TOK_TILE] = jnp.concatenate([qh, zero] if h < GROUP else [zero, qh], axis=0)
    rowb_ref[...] = lax.broadcasted_iota(I32, (tk, width), 0).astype(F32) * slope_row

    kcb = kcmp_ref[0].astype(BF16)
    vct = vcmp_ref[0].T.astype(BF16)
    c_col = lax.broadcasted_iota(I32, (n_cmp, 1), 0) * CMP_STRIDE
    dist = lane_t.astype(F32) - (c_col.astype(F32) + (CMP_LEN - 1) / 2)
    vis = c_col + (CMP_LEN - 1) <= lane_t
    j_col = lax.broadcasted_iota(I32, (n_slc, 1), 0)
    cur = lane_t // SEL_BLOCK
    forced = (j_col == 0) | (j_col == cur) | (j_col == cur - 1)
    valid = j_col * SEL_BLOCK <= lane_t
    sub = lax.broadcasted_iota(I32, (8, 1), 0)
    sc_all = jnp.dot(kcb, qp_ref[...], preferred_element_type=F32)
    probs = []
    for h in range(N_HEADS):
        sc = jnp.where(vis, head(sc_all, h) - slopes[h] * dist, NEG)
        m = jnp.max(sc, axis=0, keepdims=True)
        e = jnp.where(vis, jnp.exp(sc - m), 0.0)
        probs.append(e / jnp.maximum(jnp.sum(e, axis=0, keepdims=True), 1e-30))
    o_cmp = jnp.dot(vct, jnp.concatenate(probs, axis=1).astype(BF16), preferred_element_type=F32)
    for h in range(N_HEADS):
        g = h // GROUP
        ot_ref[HEAD_DIM * h:HEAD_DIM * (h + 1)] = gate(h, 0) * head(o_cmp[HEAD_DIM * g:HEAD_DIM * (g + 1)], h)
    for g in range(N_KV_HEADS):
        psum = probs[GROUP * g]
        for r in range(1, GROUP):
            psum = psum + probs[GROUP * g + r]
        hi = psum.astype(BF16)
        lo = (psum - hi.astype(F32)).astype(BF16)
        ovt = _overlap_t(n_slc, n_cmp)
        imp = (jnp.dot(ovt, hi, preferred_element_type=F32)
               + jnp.dot(ovt, lo, preferred_element_type=F32))
        imp = jnp.where(valid, jnp.where(forced, FORCED_SCORE, imp), -1.0)
        groups = [imp[8 * v:8 * (v + 1)] for v in range(n_slc // 8)]
        rank = [jnp.zeros((8, tq), F32) for _ in groups]
        for b in range(n_slc):
            row = imp[b:b + 1]
            for v, gv in enumerate(groups):
                if v < b // 8:
                    ahead = row > gv
                elif v > b // 8:
                    ahead = row >= gv
                else:
                    ahead = (row > gv) | ((row == gv) & (sub > b % 8))
                rank[v] = rank[v] + jnp.where(ahead, 1.0, 0.0)
        rank = jnp.concatenate(rank, axis=0)
        selb_ref[g] = jnp.where((rank < n_top) & valid, 0.0, NEG)

    half = SEL_BLOCK

    def tile_step(kt, carry, k_ref, vt_ref, use_sel, causal, window):
        m_i, l_i = carry
        s0 = pl.multiple_of(kt * tk, tk)
        k = k_ref[0, pl.ds(s0, tk), :]
        vt = vt_ref[0, kt]
        off = t0 - s0
        s = jnp.dot(k, qp_ref[...], preferred_element_type=F32) + rowb_ref[...]
        base = slope_row * (-off.astype(F32))
        if use_sel:
            blk = kt * (tk // SEL_BLOCK)
            rows = []
            for j in range(tk // SEL_BLOCK):
                sel = [selb_ref[g, pl.ds(blk + j, 1), :] for g in range(N_KV_HEADS)]
                rows.append(base + jnp.concatenate([sel[h // GROUP] for h in range(N_HEADS)], axis=1))
            s = jnp.concatenate([s[j * half:(j + 1) * half] + rows[j] for j in range(tk // SEL_BLOCK)], axis=0)
        else:
            s = s + base
        if causal or window:
            d_kq = (lax.broadcasted_iota(I32, (tk, width), 0)
                    - (lax.broadcasted_iota(I32, (tk, width), 1) & (tq - 1)))
            bad = None
            if causal:
                bad = d_kq > off
            if window:
                late = d_kq < off - WINDOW
                bad = late if bad is None else (bad | late)
            s = jnp.where(bad, NEG, s)
        m_new = jnp.maximum(m_i, jnp.max(s, axis=0, keepdims=True))
        alpha = jnp.exp(m_i - m_new)
        p = jnp.exp(s - m_new)
        l_new = alpha * l_i + jnp.sum(p, axis=0, keepdims=True)
        pb = p.astype(BF16)
        pv = jnp.concatenate(
            [jnp.dot(vt[HEAD_DIM * g:HEAD_DIM * (g + 1)], pb[:, g * gw:(g + 1) * gw], preferred_element_type=F32)
             for g in range(N_KV_HEADS)], axis=1)
        acc_ref[...] = alpha * acc_ref[...] + pv
        return m_new, l_new

    def branch(k_ref, vt_ref, use_sel, lo, mid, hi, gate_idx):
        acc_ref[...] = jnp.zeros_like(acc_ref)
        carry = (jnp.full((1, width), NEG, F32), jnp.zeros((1, width), F32))
        window = not use_sel
        carry = lax.fori_loop(lo, mid, lambda kt, c: tile_step(kt, c, k_ref, vt_ref, use_sel, False, window),
                              carry)
        carry = lax.fori_loop(mid, hi, lambda kt, c: tile_step(kt, c, k_ref, vt_ref, use_sel, True, window),
                              carry)
        out = acc_ref[...] / jnp.maximum(carry[1], 1e-30)
        for h in range(N_HEADS):
            rows = slice(HEAD_DIM * h, HEAD_DIM * (h + 1))
            ot_ref[rows] = ot_ref[rows] + gate(h, gate_idx) * head(out, h)

    first_now = t0 // tk
    end = (t0 + tq) // tk
    branch(ksb_ref, vst_ref, True, 0, first_now, end, 1)
    branch(kwb_ref, vwt_ref, False, jnp.maximum(first_now - WINDOW // tk, 0), first_now, end, 2)
    o_ref[0] = ot_ref[...].T


def _nsa_prompt(qt, gnt, kcmp, vcmp, ksb, vst, kwb, vwt, tq):
    b, n_tile, _, _ = qt.shape
    s = n_tile * TOK_TILE
    tq = min(tq, s)
    assert s % tq == 0 and tq % TOK_TILE == 0 and WINDOW % TOK_TILE == 0 and tq & (tq - 1) == 0
    n_slc = -(-s // SEL_BLOCK)
    assert n_slc % 8 == 0
    n_cmp = kcmp.shape[1]
    nq = tq // TOK_TILE
    tile = lambda n: pl.BlockSpec((1, nq, n, TOK_TILE), lambda i, j: (i, j, 0, 0))
    seq = lambda n, w: pl.BlockSpec((1, n, w), lambda i, j: (i, 0, 0))
    feat = pl.BlockSpec((1, n_tile, KV_DIM, TOK_TILE), lambda i, j: (i, 0, 0, 0))
    return pl.pallas_call(
        functools.partial(_nsa_prompt_kernel, tq=tq, n_slc=n_slc),
        grid=(b, s // tq),
        in_specs=[tile(ATTN_DIM), tile(GATE_ROWS), seq(n_cmp, KV_DIM), seq(n_cmp, KV_DIM),
                  seq(s, KV_DIM), feat, seq(s, KV_DIM), feat],
        out_specs=pl.BlockSpec((1, tq, ATTN_DIM), lambda i, j: (i, j, 0)),
        out_shape=jax.ShapeDtypeStruct((b, s, ATTN_DIM), F32),
        scratch_shapes=[pltpu.VMEM((KV_DIM, N_HEADS * tq), BF16),
                        pltpu.VMEM((N_KV_HEADS, n_slc, tq), F32),
                        pltpu.VMEM((TOK_TILE, N_HEADS * tq), F32),
                        pltpu.VMEM((HEAD_DIM, N_HEADS * tq), F32),
                        pltpu.VMEM((ATTN_DIM, tq), F32)],
        compiler_params=_params(("arbitrary", "arbitrary")),
        name="nsa_prompt",
    )(qt, gnt, kcmp, vcmp, ksb, vst, kwb, vwt)


def _query_rows(q):
    rows = []
    for h in range(N_HEADS):
        qh = q[:, HEAD_DIM * h:HEAD_DIM * (h + 1)]
        z = jnp.zeros_like(qh)
        rows.append(jnp.concatenate([qh, z] if h < GROUP else [z, qh], axis=1))
    return jnp.concatenate(rows, axis=0)


def _head_slopes():
    return jnp.concatenate([jnp.full((1, 1), 2.0 ** -(h + 1), F32) for h in range(N_HEADS)], axis=0)


def _per_group(rows_iota, v0, v1):
    return jnp.where(rows_iota < GROUP, v0, v1)


def _sample_select_kernel(q_ref, kcmp_ref, vcmp_ref, ocmp_ref, idx_ref, *, t_pos, n_slc, n_lane):
    qr = _query_rows(q_ref[0])
    n_cmp = kcmp_ref.shape[1]
    slope = _head_slopes()
    sc = _bdot_nt(qr, kcmp_ref[0])
    c_start = lax.broadcasted_iota(I32, (1, n_cmp), 1) * CMP_STRIDE
    dist = float(t_pos) - (c_start.astype(F32) + (CMP_LEN - 1) / 2)
    vis = c_start + (CMP_LEN - 1) <= t_pos
    sc = jnp.where(vis, sc - slope * dist, NEG)
    m = jnp.max(sc, axis=-1, keepdims=True)
    e = jnp.where(vis, jnp.exp(sc - m), 0.0)
    p = e / jnp.maximum(jnp.sum(e, axis=-1, keepdims=True), 1e-30)
    ocmp_ref[0] = _bdot(p, vcmp_ref[0])

    hrow = lax.broadcasted_iota(I32, (N_HEADS, 1), 0)
    psum = jnp.concatenate(
        [jnp.sum(jnp.where((hrow // GROUP) == g, p, 0.0), axis=0, keepdims=True) for g in range(N_KV_HEADS)]
        + [jnp.zeros((8 - N_KV_HEADS, n_cmp), F32)], axis=0)
    imp = _split_dot(psum, _overlap(n_cmp, n_lane))
    j_row = lax.broadcasted_iota(I32, (1, n_lane), 1)
    cur = t_pos // SEL_BLOCK
    forced = (j_row == 0) | (j_row == cur) | (j_row == cur - 1)
    valid = (j_row * SEL_BLOCK <= t_pos) & (j_row < n_slc)
    imp = jnp.where(valid, jnp.where(forced, FORCED_SCORE, imp), -1.0)
    n_top = min(SEL_TOP, n_slc)
    ii = lax.broadcasted_iota(I32, (n_lane, n_lane), 0)
    jj = lax.broadcasted_iota(I32, (n_lane, n_lane), 1)
    slot = lax.broadcasted_iota(I32, (1, SEL_TOP), 1)
    rows = []
    for g in range(N_KV_HEADS):
        by_lane = jnp.broadcast_to(imp[g:g + 1], (n_lane, n_lane))
        col = jnp.sum(jnp.where(ii == jj, by_lane, 0.0), axis=1, keepdims=True)
        by_row = jnp.broadcast_to(col, (n_lane, n_lane))
        beaten = (by_lane > by_row) | ((by_lane == by_row) & (jj < ii))
        rank_col = jnp.sum(beaten.astype(F32), axis=1, keepdims=True)
        hit = rank_col == slot.astype(F32)
        block = jnp.sum(jnp.where(hit, ii[:, :SEL_TOP].astype(F32), 0.0), axis=0, keepdims=True)
        rows.append(jnp.where(slot < n_top, block, float(n_lane - 1)).astype(I32))
    idx_ref[0] = jnp.concatenate(rows, axis=0)


def _sample_select(q, kcmp, vcmp, t_pos):
    db = q.shape[0]
    n_cmp = kcmp.shape[1]
    n_slc = -(-(t_pos + 1) // SEL_BLOCK)
    n_lane = -(-n_slc // 128) * 128
    return pl.pallas_call(
        functools.partial(_sample_select_kernel, t_pos=t_pos, n_slc=n_slc, n_lane=n_lane),
        grid=(db,),
        in_specs=[pl.BlockSpec((1, 1, ATTN_DIM), lambda i: (i, 0, 0)),
                  pl.BlockSpec((1, n_cmp, KV_DIM), lambda i: (i, 0, 0)),
                  pl.BlockSpec((1, n_cmp, KV_DIM), lambda i: (i, 0, 0))],
        out_specs=[pl.BlockSpec((1, N_HEADS, KV_DIM), lambda i: (i, 0, 0)),
                   pl.BlockSpec((1, N_KV_HEADS, SEL_TOP), lambda i: (i, 0, 0))],
        out_shape=[jax.ShapeDtypeStruct((db, N_HEADS, KV_DIM), F32),
                   jax.ShapeDtypeStruct((db, N_KV_HEADS, SEL_TOP), I32)],
        compiler_params=_params(("arbitrary",)),
        name="sample_select",
    )(q, kcmp, vcmp)


def _sample_attend_kernel(idx_s, pt_s, q_ref, idx_ref, ksn_ref, vsn_ref, kwb_ref, vwb_ref, kwn_ref, vwn_ref,
                          gn_ref, ocmp_ref, kpool, vpool, o_ref, kbuf, vbuf, sem, *, t_pos, n_past_blk):
    b = pl.program_id(0)
    nb = pl.num_programs(0)
    slot = b % 2
    n_sel = N_KV_HEADS * SEL_TOP
    blk_per_page = PAGE_ROWS // SEL_BLOCK

    def block_copy(pool, buf, seq, s, sl, which):
        blk = jnp.minimum(idx_s[seq * n_sel + s], n_past_blk - 1)
        page = pt_s[seq, blk // blk_per_page]
        off = pl.multiple_of((blk % blk_per_page) * SEL_BLOCK, SEL_BLOCK)
        return pltpu.make_async_copy(pool.at[page, pl.ds(off, SEL_BLOCK)], buf.at[sl, s], sem.at[which, sl])

    def fetch(seq, sl):
        for s in range(n_sel):
            block_copy(kpool, kbuf, seq, s, sl, 0).start()
            block_copy(vpool, vbuf, seq, s, sl, 1).start()

    @pl.when(b == 0)
    def _():
        fetch(0, 0)

    @pl.when(b + 1 < nb)
    def _():
        fetch(b + 1, 1 - slot)

    for s in range(n_sel):
        block_copy(kpool, kbuf, b, s, slot, 0).wait()
        block_copy(vpool, vbuf, b, s, slot, 1).wait()

    qr = _query_rows(q_ref[0])
    slope = _head_slopes()
    hrow = lax.broadcasted_iota(I32, (N_HEADS, 1), 0)

    def with_new_key(s, mask, v, s_new, new_ok, v_new):
        s = jnp.where(mask, s, NEG)
        s_new = jnp.where(new_ok, s_new, NEG)
        m = jnp.maximum(jnp.max(s, axis=-1, keepdims=True), s_new)
        e = jnp.where(mask, jnp.exp(s - m), 0.0)
        e_new = jnp.where(new_ok, jnp.exp(s_new - m), 0.0)
        l = jnp.sum(e, axis=-1, keepdims=True) + e_new
        return (_bdot(e, v) + e_new * v_new) / jnp.maximum(l, 1e-30)

    n_col = SEL_TOP * SEL_BLOCK
    k_all = kbuf[slot].reshape(n_sel * SEL_BLOCK, KV_DIM)
    v_all = vbuf[slot].reshape(n_sel * SEL_BLOCK, KV_DIM)
    idx8 = jnp.concatenate([idx_ref[0].astype(F32), jnp.zeros((8 - N_KV_HEADS, SEL_TOP), F32)], axis=0)
    expand = (lax.broadcasted_iota(I32, (SEL_TOP, n_col), 0)
              == lax.broadcasted_iota(I32, (SEL_TOP, n_col), 1) // SEL_BLOCK)
    blk8 = _bdot(idx8, expand.astype(BF16))
    blk = jnp.concatenate([blk8[g:g + 1] for g in range(N_KV_HEADS)], axis=1)
    col = lax.broadcasted_iota(I32, (1, N_KV_HEADS * n_col), 1)
    pos = blk * SEL_BLOCK + (col % SEL_BLOCK).astype(F32)
    own = (col // n_col) == (hrow // GROUP)
    mask = own & (blk < n_past_blk)
    s_sel = _bdot_nt(qr, k_all) - slope * (float(t_pos) - pos)
    cur = float(t_pos // SEL_BLOCK)
    has_new = [jnp.max(jnp.where(idx8[g:g + 1] == cur, 1.0, 0.0), axis=-1, keepdims=True)
               for g in range(N_KV_HEADS)]
    new_ok = _per_group(hrow, has_new[0], has_new[1]) > 0.5
    s_new = jnp.sum(qr * ksn_ref[0], axis=-1, keepdims=True)
    o_sel = with_new_key(s_sel, mask, v_all, s_new, new_ok, vsn_ref[0])

    w_buf = kwb_ref.shape[1]
    delta = w_buf - lax.broadcasted_iota(I32, (1, w_buf), 1)
    wmask = (delta <= WINDOW) & (t_pos - delta >= 0)
    s_win = _bdot_nt(qr, kwb_ref[0]) - slope * delta.astype(F32)
    s_wnew = jnp.sum(qr * kwn_ref[0], axis=-1, keepdims=True)
    o_win = with_new_key(s_win, jnp.broadcast_to(wmask, s_win.shape), vwb_ref[0], s_wnew, hrow >= 0, vwn_ref[0])

    lane = lax.broadcasted_iota(I32, (N_HEADS, GATE_PAD), 1)
    gn = jnp.broadcast_to(gn_ref[0], (N_HEADS, GATE_PAD))
    gate = lambda j: jnp.sum(jnp.where(lane == 3 * hrow + j, gn, 0.0), axis=-1, keepdims=True)
    o = gate(0) * ocmp_ref[0] + gate(1) * o_sel + gate(2) * o_win
    o_ref[0] = jnp.concatenate(
        [o[h:h + 1, HEAD_DIM * (h // GROUP):HEAD_DIM * (h // GROUP + 1)] for h in range(N_HEADS)], axis=1)


def _sample_attend(idx, page_table, q, ks_new, vs_new, kw_buf, vw_buf, kw_new, vw_new, gn, ocmp,
                   ks_pool, vs_pool, t_pos):
    db = q.shape[0]
    w_buf = kw_buf.shape[1]
    n_sel = N_KV_HEADS * SEL_TOP
    n_past_blk = page_table.shape[1] * (PAGE_ROWS // SEL_BLOCK)
    row = lambda n: pl.BlockSpec((1, 1, n), lambda i, a, b: (i, 0, 0))
    grid_spec = pltpu.PrefetchScalarGridSpec(
        num_scalar_prefetch=2,
        grid=(db,),
        in_specs=[row(ATTN_DIM),
                  pl.BlockSpec((1, N_KV_HEADS, SEL_TOP), lambda i, a, b: (i, 0, 0)),
                  row(KV_DIM), row(KV_DIM),
                  pl.BlockSpec((1, w_buf, KV_DIM), lambda i, a, b: (i, 0, 0)),
                  pl.BlockSpec((1, w_buf, KV_DIM), lambda i, a, b: (i, 0, 0)),
                  row(KV_DIM), row(KV_DIM), row(GATE_PAD),
                  pl.BlockSpec((1, N_HEADS, KV_DIM), lambda i, a, b: (i, 0, 0)),
                  pl.BlockSpec(memory_space=pl.ANY), pl.BlockSpec(memory_space=pl.ANY)],
        out_specs=row(ATTN_DIM),
        scratch_shapes=[pltpu.VMEM((2, n_sel, SEL_BLOCK, KV_DIM), F32),
                        pltpu.VMEM((2, n_sel, SEL_BLOCK, KV_DIM), F32),
                        pltpu.SemaphoreType.DMA((2, 2))],
    )
    return pl.pallas_call(
        functools.partial(_sample_attend_kernel, t_pos=t_pos, n_past_blk=n_past_blk),
        grid_spec=grid_spec,
        out_shape=jax.ShapeDtypeStruct((db, 1, ATTN_DIM), F32),
        compiler_params=_params(("arbitrary",)),
        name="sample_attend",
    )(idx.reshape(-1), page_table, q, idx, ks_new, vs_new, kw_buf, vw_buf, kw_new, vw_new, gn, ocmp,
      ks_pool, vs_pool)


def _merge_kernel(x_ref, oatt_ref, u_ref, hist_ref, bg_ref, ma_ref, mc_ref, cw_ref,
                  wao_ref, wco_ref, wo_ref, gate1_ref, shift2_ref, scale2_ref, g2_ref,
                  x1_ref, h2_ref, *, per_token_history):
    u = u_ref[0]
    tm = u.shape[0]
    if per_token_history:
        u2, u1 = hist_ref[0], hist_ref[1]
    else:
        prev = hist_ref[0]
        first = pl.program_id(1) == 0
        p1 = jnp.where(first, 0.0, prev[7:8])
        p2 = jnp.where(first, 0.0, prev[6:7])
        r = lax.broadcasted_iota(I32, (tm, 1), 0)
        u1 = jnp.where(r == 0, p1, pltpu.roll(u, 1, 0))
        u2 = jnp.where(r == 0, p2, jnp.where(r == 1, p1, pltpu.roll(u, 2, 0)))
    cw = cw_ref[...]
    conv = u2 * cw[0:1] + u1 * cw[1:2] + u * cw[2:3]
    y_conv = bg_ref[0] * conv
    mix = ma_ref[0] * _bdot(oatt_ref[0], wao_ref[...]) + mc_ref[0] * _bdot(y_conv, wco_ref[...])
    x1 = x_ref[0] + gate1_ref[0] * _bdot(mix, wo_ref[...])
    x1_ref[0] = x1
    h2_ref[0] = _rms_mod(x1, g2_ref[...], shift2_ref[0], scale2_ref[0])


def _merge(x, oatt, u, hist, bg, ma, mc, conv_w, wao, wco, wo, gate1, shift2, scale2, g2, tm):
    b, s, d = x.shape
    tm = min(tm, s)
    per_token = hist is not None
    r = gate1.shape[1]
    rb = 1 if r == 1 else tm
    mod_map = (lambda i, j: (i, 0, 0)) if r == 1 else (lambda i, j: (i, j, 0))
    row = lambda n: pl.BlockSpec((1, tm, n), lambda i, j: (i, j, 0))
    const = lambda a: pl.BlockSpec(a.shape, lambda i, j: (0, 0))
    mod = pl.BlockSpec((1, rb, d), mod_map)
    if per_token:
        assert b == 1
        hist_arr = hist
        hist_spec = pl.BlockSpec((2, tm, CONV_DIM), lambda i, j: (0, j, 0))
    else:
        hist_arr = u
        hist_spec = pl.BlockSpec((1, 8, CONV_DIM), lambda i, j: (i, jnp.maximum(j * (tm // 8) - 1, 0), 0))
    return pl.pallas_call(
        functools.partial(_merge_kernel, per_token_history=per_token),
        grid=(b, s // tm),
        in_specs=[row(d), row(ATTN_DIM), row(CONV_DIM), hist_spec, row(CONV_DIM), row(d), row(d),
                  const(conv_w), const(wao), const(wco), const(wo), mod, mod, mod, const(g2)],
        out_specs=[row(d), row(d)],
        out_shape=[jax.ShapeDtypeStruct((b, s, d), F32)] * 2,
        compiler_params=_params(("arbitrary", "arbitrary")),
        name="merge",
    )(x, oatt, u, hist_arr, bg, ma, mc, conv_w, wao, wco, wo, gate1, shift2, scale2, g2)


def _route_kernel(h_ref, rwt_ref, rb_ref, e_ref, w_ref, pos_ref, cnt_ref):
    @pl.when(pl.program_id(0) == 0)
    def _():
        cnt_ref[...] = jnp.zeros_like(cnt_ref)

    tm = h_ref.shape[0]
    per_group = N_EXPERTS // N_GROUPS
    aff = _sigmoid(_bdot_nt(rwt_ref[...], h_ref[...]))
    biased = aff + rb_ref[...]
    row = lax.broadcasted_iota(I32, (N_EXPERTS, tm), 0)
    lrow = lax.broadcasted_iota(I32, (per_group, tm), 0)
    ninf = -jnp.inf

    score = []
    for g in range(N_GROUPS):
        xg = biased[g * per_group:(g + 1) * per_group]
        m1 = jnp.max(xg, axis=0, keepdims=True)
        i1 = jnp.min(jnp.where(xg == m1, lrow, per_group), axis=0, keepdims=True)
        m2 = jnp.max(jnp.where(lrow == i1, ninf, xg), axis=0, keepdims=True)
        score.append(m1 + m2)
    cand = []
    for a in range(N_GROUPS):
        ahead = jnp.zeros((1, tm), F32)
        for b in range(N_GROUPS):
            if b != a:
                wins = (score[b] > score[a]) | ((score[b] == score[a]) & (b < a))
                ahead = ahead + wins.astype(F32)
        cand.append(jnp.where(ahead < TOPK_GROUPS, biased[a * per_group:(a + 1) * per_group], ninf))
    cand = jnp.concatenate(cand, axis=0)

    chosen = jnp.zeros((N_EXPERTS, tm), F32)
    e_rows, w_rows = [], []
    for _ in range(TOP_K):
        m = jnp.max(cand, axis=0, keepdims=True)
        idx = jnp.min(jnp.where(cand == m, row, N_EXPERTS), axis=0, keepdims=True)
        hit = row == idx
        e_rows.append(idx)
        w_rows.append(jnp.sum(jnp.where(hit, aff, 0.0), axis=0, keepdims=True))
        cand = jnp.where(hit, ninf, cand)
        chosen = chosen + hit.astype(F32)
    total = w_rows[0]
    for w in w_rows[1:]:
        total = total + w
    w_rows = [w / total * ROUTED_SCALE for w in w_rows]

    earlier = (lax.broadcasted_iota(I32, (tm, tm), 0) < lax.broadcasted_iota(I32, (tm, tm), 1)).astype(BF16)
    before = _bdot(chosen, earlier) + cnt_ref[...]
    pos_rows = [jnp.sum(jnp.where(row == e, before, 0.0), axis=0, keepdims=True) for e in e_rows]
    cnt_ref[...] += jnp.sum(chosen, axis=1, keepdims=True)

    e_ref[...] = jnp.concatenate(e_rows, axis=0)
    w_ref[...] = jnp.concatenate(w_rows, axis=0)
    pos_ref[...] = jnp.concatenate(pos_rows, axis=0).astype(I32)


def _route(h, rwt, rb, tm):
    n, d = h.shape
    tm = min(tm, n)
    slot = pl.BlockSpec((TOP_K, tm), lambda i: (0, i))
    return pl.pallas_call(
        _route_kernel,
        grid=(n // tm,),
        in_specs=[pl.BlockSpec((tm, d), lambda i: (i, 0)),
                  pl.BlockSpec((N_EXPERTS, d), lambda i: (0, 0)),
                  pl.BlockSpec((N_EXPERTS, 1), lambda i: (0, 0))],
        out_specs=[slot, slot, slot, pl.BlockSpec((N_EXPERTS, 1), lambda i: (0, 0))],
        out_shape=[jax.ShapeDtypeStruct((TOP_K, n), I32), jax.ShapeDtypeStruct((TOP_K, n), F32),
                   jax.ShapeDtypeStruct((TOP_K, n), I32), jax.ShapeDtypeStruct((N_EXPERTS, 1), F32)],
        compiler_params=_params(("arbitrary",)),
        name="moe_route",
    )(h, rwt, rb)


def _dest_kernel(e_ref, pos_ref, start_ref, d_ref):
    tm = e_ref.shape[1]
    row = lax.broadcasted_iota(I32, (N_EXPERTS, tm), 0)
    start = start_ref[...]
    rows = []
    for k in range(TOP_K):
        base = jnp.sum(jnp.where(row == e_ref[k:k + 1], start, 0.0), axis=0, keepdims=True)
        rows.append(base.astype(I32) + pos_ref[k:k + 1])
    d_ref[...] = jnp.concatenate(rows, axis=0)


def _dest(e_t, pos_t, start, tm):
    n = e_t.shape[1]
    tm = min(tm, n)
    slot = pl.BlockSpec((TOP_K, tm), lambda i: (0, i))
    return pl.pallas_call(
        _dest_kernel,
        grid=(n // tm,),
        in_specs=[slot, slot, pl.BlockSpec((N_EXPERTS, 1), lambda i: (0, 0))],
        out_specs=slot,
        out_shape=jax.ShapeDtypeStruct((TOP_K, n), I32),
        compiler_params=_params(("arbitrary",)),
        name="moe_dest",
    )(e_t, pos_t, start)


def _dispatch_kernel(dest_ref, h_ref, xs_in, xs_out, sem):
    del xs_in
    tm = h_ref.shape[0]

    def row_copy(n, d):
        return pltpu.make_async_copy(h_ref.at[pl.ds(n, 1)], xs_out.at[pl.ds(d, 1)], sem)

    def issue(n, c):
        for k in range(TOP_K):
            row_copy(n, dest_ref[n * TOP_K + k]).start()
        return c

    lax.fori_loop(0, tm, issue, 0)
    for k in range(TOP_K):
        pltpu.make_async_copy(h_ref, xs_out.at[pl.ds(0, tm)], sem).wait()


def _dispatch(dest_flat, h, n_rows, tm):
    n, d = h.shape
    tm = min(tm, n)
    xs = jnp.zeros((n_rows, d), F32)
    return pl.pallas_call(
        _dispatch_kernel,
        grid=(n // tm,),
        in_specs=[pl.BlockSpec((tm * TOP_K,), lambda i: (i,), memory_space=pltpu.SMEM),
                  pl.BlockSpec((tm, d), lambda i: (i, 0)),
                  pl.BlockSpec(memory_space=pl.ANY)],
        out_specs=pl.BlockSpec(memory_space=pl.ANY),
        out_shape=jax.ShapeDtypeStruct((n_rows, d), F32),
        scratch_shapes=[pltpu.SemaphoreType.DMA(())],
        input_output_aliases={2: 0},
        compiler_params=_params(("arbitrary",)),
        name="moe_dispatch",
    )(dest_flat, h, xs)


def _expert_kernel(be_ref, used_ref, x_ref, wg_ref, wu_ref, wd_ref, o_ref):
    i = pl.program_id(0)

    @pl.when(i < used_ref[0])
    def _():
        x = x_ref[...].astype(BF16)
        g = jnp.dot(x, wg_ref[0], preferred_element_type=F32)
        u = jnp.dot(x, wu_ref[0], preferred_element_type=F32)
        o_ref[...] = jnp.dot((_silu(g) * u).astype(BF16), wd_ref[0], preferred_element_type=F32)

    @pl.when(i >= used_ref[0])
    def _():
        o_ref[...] = jnp.zeros_like(o_ref)


def _experts(block_e, n_used, xs, wg, wu, wd):
    n_rows, d = xs.shape
    ff = wg.shape[2]
    grid_spec = pltpu.PrefetchScalarGridSpec(
        num_scalar_prefetch=2,
        grid=(n_rows // MOE_ROWS,),
        in_specs=[pl.BlockSpec((MOE_ROWS, d), lambda i, be, nu: (i, 0)),
                  pl.BlockSpec((1, d, ff), lambda i, be, nu: (be[i], 0, 0)),
                  pl.BlockSpec((1, d, ff), lambda i, be, nu: (be[i], 0, 0)),
                  pl.BlockSpec((1, ff, d), lambda i, be, nu: (be[i], 0, 0))],
        out_specs=pl.BlockSpec((MOE_ROWS, d), lambda i, be, nu: (i, 0)),
    )
    return pl.pallas_call(
        _expert_kernel,
        grid_spec=grid_spec,
        out_shape=jax.ShapeDtypeStruct((n_rows, d), F32),
        compiler_params=_params(("arbitrary",)),
        name="moe_experts",
    )(block_e, n_used, xs, wg, wu, wd)


def _combine_kernel(dest_ref, w_ref, x1_ref, h2_ref, gate2_ref, sg_ref, su_ref, sd_ref, ys_hbm,
                    o_ref, gbuf, sem):
    tm = x1_ref.shape[0]

    def row_copy(n, k, d):
        return pltpu.make_async_copy(ys_hbm.at[pl.ds(d, 1)], gbuf.at[k, pl.ds(n, 1)], sem)

    def issue(n, c):
        for k in range(TOP_K):
            row_copy(n, k, dest_ref[n * TOP_K + k]).start()
        return c

    lax.fori_loop(0, tm, issue, 0)
    h = h2_ref[...].astype(BF16)
    g = jnp.dot(h, sg_ref[...], preferred_element_type=F32)
    u = jnp.dot(h, su_ref[...], preferred_element_type=F32)
    y = jnp.dot((_silu(g) * u).astype(BF16), sd_ref[...], preferred_element_type=F32)
    for k in range(TOP_K):
        pltpu.make_async_copy(ys_hbm.at[pl.ds(0, tm)], gbuf.at[k], sem).wait()
    w = w_ref[...]
    for k in range(TOP_K):
        y = y + w[:, k:k + 1] * gbuf[k]
    o_ref[...] = x1_ref[...] + gate2_ref[0] * y


def _combine(dest_flat, w, x1, h2, gate2, tokens_per_gate, sg, su, sd, ys, tm):
    n, d = x1.shape
    tm = min(tm, n)
    row = pl.BlockSpec((tm, d), lambda i: (i, 0))
    const = lambda a: pl.BlockSpec(a.shape, lambda i: (0, 0))
    if gate2.shape[1] == 1:
        gate_spec = pl.BlockSpec((1, 1, d), lambda i: ((i * tm) // tokens_per_gate, 0, 0))
    else:
        gate_spec = pl.BlockSpec((1, tm, d), lambda i: (0, i, 0))
    return pl.pallas_call(
        _combine_kernel,
        grid=(n // tm,),
        in_specs=[pl.BlockSpec((tm * TOP_K,), lambda i: (i,), memory_space=pltpu.SMEM),
                  pl.BlockSpec((tm, TOP_K), lambda i: (i, 0)),
                  row, row, gate_spec, const(sg), const(su), const(sd),
                  pl.BlockSpec(memory_space=pl.ANY)],
        out_specs=row,
        out_shape=jax.ShapeDtypeStruct((n, d), F32),
        scratch_shapes=[pltpu.VMEM((TOP_K, tm, d), F32), pltpu.SemaphoreType.DMA(())],
        compiler_params=_params(("arbitrary",)),
        name="moe_combine",
    )(dest_flat, w, x1, h2, gate2, sg, su, sd, ys)


def _moe(h2, x1, gate2, tokens_per_gate, mw):
    n = h2.shape[0]
    e_t, w_t, pos_t, counts = _route(h2, mw["rwt"], mw["rb"], 256)
    counts = counts.reshape(-1).astype(I32)
    padded = (counts + MOE_ROWS - 1) // MOE_ROWS * MOE_ROWS
    pad_end = jnp.cumsum(padded)
    n_blocks = n * TOP_K // MOE_ROWS + N_EXPERTS
    start = (pad_end - padded).astype(F32).reshape(N_EXPERTS, 1)
    dest = _dest(e_t, pos_t, start, 512).T.reshape(-1)
    block_e = jnp.minimum(jnp.searchsorted(pad_end, jnp.arange(n_blocks, dtype=I32) * MOE_ROWS, side="right"),
                          N_EXPERTS - 1).astype(I32)
    n_used = (pad_end[-1:] // MOE_ROWS).astype(I32)
    xs = _dispatch(dest, h2, n_blocks * MOE_ROWS, 256)
    ys = _experts(block_e, n_used, xs, mw["wg"], mw["wu"], mw["wd"])
    return _combine(dest, w_t.T, x1, h2, gate2, tokens_per_gate, mw["sg"], mw["su"], mw["sd"], ys, 128)


def _pack_w_in(w_in):
    n_gate = N_HEADS * 3
    kv0 = ATTN_DIM
    g0 = kv0 + 6 * KV_DIM
    w_rows = jnp.concatenate([w_in[:, kv0:g0], w_in[:, g0 + n_gate:]], axis=1).astype(BF16)
    gate = jnp.pad(w_in[:, g0:g0 + n_gate], ((0, 0), (0, GATE_ROWS - n_gate)))
    v_sel = w_in[:, kv0 + 3 * KV_DIM:kv0 + 4 * KV_DIM]
    v_win = w_in[:, kv0 + 5 * KV_DIM:kv0 + 6 * KV_DIM]
    w_feat = jnp.concatenate([w_in[:, :ATTN_DIM], v_sel, v_win, gate], axis=1).T.astype(BF16)
    return w_rows, w_feat


def _compress_weights(cmp_pe, cmp_w1, cmp_w2, g_kcmp):
    eye = jnp.eye(N_KV_HEADS, dtype=F32)
    sub = CMP_LEN // CMP_STRIDE

    def w1_cat(w1):
        w = w1.reshape(sub, CMP_STRIDE, HEAD_DIM, CMP_HID)
        mats = [jnp.einsum("sdh,gk->sgdkh", w[j], eye).reshape(CMP_STRIDE * KV_DIM, N_KV_HEADS * CMP_HID)
                for j in range(sub)]
        return jnp.concatenate(mats, axis=1).astype(BF16)

    def w2_blk(w2):
        return jnp.einsum("hd,gk->ghkd", w2, eye).reshape(N_KV_HEADS * CMP_HID, KV_DIM).astype(BF16)

    def pe_rows(pe):
        p = pe.reshape(sub, CMP_STRIDE, 1, HEAD_DIM)
        p = jnp.broadcast_to(p, (sub, CMP_STRIDE, N_KV_HEADS, HEAD_DIM)).reshape(sub, 1, CMP_STRIDE * KV_DIM)
        return jnp.broadcast_to(p, (sub, 8, CMP_STRIDE * KV_DIM))

    assert sub == 2
    return {
        "w1k": w1_cat(cmp_w1[0]), "w1v": w1_cat(cmp_w1[1]),
        "w2k": w2_blk(cmp_w2[0]), "w2v": w2_blk(cmp_w2[1]),
        "pe": jnp.concatenate([pe_rows(cmp_pe[0]), pe_rows(cmp_pe[1])], axis=0),
        "gk": jnp.tile(g_kcmp, N_KV_HEADS).reshape(1, KV_DIM),
    }


def _layer(l, x_p, x_s, c_all, caches, page_table, w):
    ckc, cvc, cks, cvs, ckw, cvw, sconv = caches
    b, s, d = x_p.shape
    db, ds = x_s.shape[:2]
    assert ds == 1, "one new token per sample sequence"
    n_pool = ckc.shape[1]
    t_pos = page_table.shape[1] * PAGE_ROWS
    hist = CONV_WIDTH - 1

    ada = _ada(c_all, w["w_ada"][l], w["b_ada"][l])
    mods = [ada[:, k * d:(k + 1) * d] for k in range(6)]
    mod_p = [m[:b, None] for m in mods]
    mod_s = [m[b:][None] for m in mods]

    qkg = w["qk_norm_g"][l]
    gq = jnp.tile(qkg[0], N_HEADS).reshape(-1, 1)
    gks = jnp.tile(qkg[2], N_KV_HEADS).reshape(1, -1)
    gkw = jnp.tile(qkg[3], N_KV_HEADS).reshape(1, -1)
    g1 = w["norm1_g"][l].reshape(1, d)
    g2 = w["norm2_g"][l].reshape(1, d)
    w_rows, w_feat = _pack_w_in(w["w_in"][l])
    cw = _compress_weights(w["cmp_pe"][l], w["cmp_w1"][l], w["cmp_w2"][l], qkg[1])
    wao, wco, wo = (w[k][l].astype(BF16) for k in ("w_attn_out", "w_conv_out", "w_o"))
    mw = {"rwt": w["router_w"][l].T.astype(BF16), "rb": w["router_b"][l].reshape(-1, 1),
          "wg": w["exp_w_gate"][l].astype(BF16), "wu": w["exp_w_up"][l].astype(BF16),
          "wd": w["exp_w_down"][l].astype(BF16), "sg": w["shared_w_gate"][l].astype(BF16),
          "su": w["shared_w_up"][l].astype(BF16), "sd": w["shared_w_down"][l].astype(BF16)}
    pages = lambda a: a.reshape(-1, PAGE_ROWS, KV_DIM)

    (qt, kc, vc, ks, vs, kw, vw, ksb, kwb, vst, vwt, gnt, u, bg, ma, mc) = _in_proj(
        x_p, mod_p[0], mod_p[1], g1, w_rows, w_feat, gq, gks, gkw, 256)
    own_pages = jnp.arange(b * s // PAGE_ROWS, dtype=I32).reshape(b, s // PAGE_ROWS)
    kcmp, vcmp = _compress(own_pages, pages(kc), pages(vc), cw)
    o_att = _nsa_prompt(qt, gnt, kcmp, vcmp, ksb, vst, kwb, vwt, 256)
    x1, h2 = _merge(x_p, o_att, u, None, bg, ma, mc, w["conv_w"][l], wao, wco, wo,
                    mod_p[2], mod_p[3], mod_p[4], g2, 256)
    y_p = _moe(h2.reshape(b * s, d), x1.reshape(b * s, d), mod_p[5], s, mw).reshape(b, s, d)
    keep = min(WINDOW, s)
    heads = lambda a: a.reshape(a.shape[0], a.shape[1], N_KV_HEADS, HEAD_DIM)
    p_state = (heads(kc), heads(vc), heads(ks), heads(vs), heads(kw[:, -keep:]), heads(vw[:, -keep:]),
               u[:, -hist:])

    (qt, kc, vc, ks, vs, kw, vw, _, _, _, _, gnt, u, bg, ma, mc) = _in_proj(
        x_s.reshape(1, db, d), mod_s[0], mod_s[1], g1, w_rows, w_feat, gq, gks, gkw, 128)
    per_seq = lambda a: a.reshape(db, 1, a.shape[-1])
    token_major = lambda a: jnp.swapaxes(a[0], 1, 2).reshape(db, a.shape[2]).astype(F32)
    q = token_major(qt)
    gn = jnp.pad(token_major(gnt), ((0, 0), (0, GATE_PAD - GATE_ROWS)))
    kcmp, vcmp = _compress(page_table, pages(ckc[l]), pages(cvc[l]), cw)
    o_cmp, idx = _sample_select(per_seq(q), kcmp, vcmp, t_pos)
    w_buf = ckw.shape[2]
    o_att = _sample_attend(idx, page_table, per_seq(q), per_seq(ks), per_seq(vs),
                           ckw[l].reshape(db, w_buf, KV_DIM), cvw[l].reshape(db, w_buf, KV_DIM),
                           per_seq(kw), per_seq(vw), per_seq(gn), o_cmp, pages(cks[l]), pages(cvs[l]), t_pos)
    x1, h2 = _merge(x_s.reshape(1, db, d), o_att.reshape(1, db, ATTN_DIM), u, jnp.swapaxes(sconv[l], 0, 1),
                    bg, ma, mc, w["conv_w"][l], wao, wco, wo, mod_s[2], mod_s[3], mod_s[4], g2, 128)
    y_s = _moe(h2.reshape(db, d), x1.reshape(db, d), mod_s[5], 1, mw).reshape(db, 1, d)
    new_row = lambda a: a.reshape(db, 1, N_KV_HEADS, HEAD_DIM)
    s_state = (new_row(kc), new_row(vc), new_row(ks), new_row(vs),
               jnp.concatenate([ckw[l], new_row(kw)], axis=1)[:, -w_buf:],
               jnp.concatenate([cvw[l], new_row(vw)], axis=1)[:, -w_buf:],
               jnp.concatenate([sconv[l], u.reshape(db, 1, CONV_DIM)], axis=1)[:, -hist:])
    return y_p, y_s, p_state + s_state


def kernel(x_prompt, x_sample, cache_k_cmp, cache_v_cmp, cache_k_sel, cache_v_sel, cache_k_win, cache_v_win,
           state_conv, page_table, c_prompt, c_sample, w_ada, b_ada, norm1_g, norm2_g, w_in, qk_norm_g,
           cmp_pe, cmp_w1, cmp_w2, conv_w, w_attn_out, w_conv_out, w_o, router_w, router_b,
           exp_w_gate, exp_w_up, exp_w_down, shared_w_gate, shared_w_up, shared_w_down):
    w = dict(w_ada=w_ada, b_ada=b_ada, norm1_g=norm1_g, norm2_g=norm2_g, w_in=w_in, qk_norm_g=qk_norm_g,
             cmp_pe=cmp_pe, cmp_w1=cmp_w1, cmp_w2=cmp_w2, conv_w=conv_w, w_attn_out=w_attn_out,
             w_conv_out=w_conv_out, w_o=w_o, router_w=router_w, router_b=router_b, exp_w_gate=exp_w_gate,
             exp_w_up=exp_w_up, exp_w_down=exp_w_down, shared_w_gate=shared_w_gate, shared_w_up=shared_w_up,
             shared_w_down=shared_w_down)
    caches = (cache_k_cmp, cache_v_cmp, cache_k_sel, cache_v_sel, cache_k_win, cache_v_win, state_conv)
    c_all = jnp.concatenate([c_prompt, c_sample], axis=0)
    x_p, x_s = x_prompt, x_sample
    states = []
    for l in range(w_ada.shape[0]):
        x_p, x_s, st = _layer(l, x_p, x_s, c_all, caches, page_table.astype(I32), w)
        states.append(st)
    return (x_p, x_s) + tuple(jnp.stack(s) for s in zip(*states))
```

```python
import functools

import jax
import jax.numpy as jnp
from jax import lax
from jax.experimental import pallas as pl
from jax.experimental.pallas import tpu as pltpu

F32 = jnp.float32
BF16 = jnp.bfloat16
I32 = jnp.int32

D_MODEL = 1024
N_HEADS = 8
HEAD_DIM = 64
N_KV_HEADS = 2
GROUP = N_HEADS // N_KV_HEADS
ATTN_DIM = N_HEADS * HEAD_DIM
KV_DIM = N_KV_HEADS * HEAD_DIM
ATTN_SCALE = HEAD_DIM ** -0.5
CMP_LEN = 32
CMP_STRIDE = 16
CMP_HID = 4 * HEAD_DIM
SEL_BLOCK = 64
SEL_TOP = 16
FORCED_SCORE = 1e6
WINDOW = 512
CONV_DIM = D_MODEL // 2
CONV_WIDTH = 3
N_EXPERTS = 256
TOP_K = 8
N_GROUPS = 8
TOPK_GROUPS = 4
EXPERT_FF = D_MODEL // 4
ROUTED_SCALE = 2.5
NORM_EPS = 1e-6
PAGE_ROWS = 128
CHUNKS_PER_PAGE = PAGE_ROWS // CMP_STRIDE
NEG = -1e30
GATE_PAD = 128
MOE_ROWS = 256
VMEM_LIMIT = 56 * 1024 * 1024


def _params(sem):
    return pltpu.CompilerParams(dimension_semantics=sem, vmem_limit_bytes=VMEM_LIMIT)


def _bdot(a, b):
    return jnp.dot(a.astype(BF16), b.astype(BF16), preferred_element_type=F32)


def _bdot_nt(a, b):
    return lax.dot_general(a.astype(BF16), b.astype(BF16), (((1,), (1,)), ((), ())),
                           preferred_element_type=F32)


def _split_dot(a, b_bf16, nt=False):
    hi = a.astype(BF16)
    lo = (a - hi.astype(F32)).astype(BF16)
    f = _bdot_nt if nt else _bdot
    return f(hi, b_bf16) + f(lo, b_bf16)


def _sigmoid(x):
    return 1.0 / (1.0 + jnp.exp(-x))


def _silu(x):
    return x * _sigmoid(x)


def _gelu_tanh(x):
    return 0.5 * x * (1.0 + jnp.tanh(0.7978845608028654 * (x + 0.044715 * (x * x * x))))


def _head_rms(z, gain):
    r = lax.broadcasted_iota(I32, (128, 128), 0) // HEAD_DIM
    c = lax.broadcasted_iota(I32, (128, 128), 1) // HEAD_DIM
    seg = (r == c).astype(BF16)
    x2 = z * z
    parts = [_split_dot(x2[:, j:j + 128], seg) for j in range(0, z.shape[1], 128)]
    ss = parts[0] if len(parts) == 1 else jnp.concatenate(parts, axis=1)
    return z * lax.rsqrt(ss * (1.0 / HEAD_DIM) + NORM_EPS) * gain


def _rms_mod(x, g, shift, scale):
    y = x * lax.rsqrt(jnp.mean(x * x, axis=-1, keepdims=True) + NORM_EPS) * g
    return y * (1.0 + scale) + shift


def _ada_kernel(c_ref, w_ref, b_ref, o_ref):
    o_ref[...] = _bdot(_silu(c_ref[...]), w_ref[...]) + b_ref[...]


def _ada(c, w, b):
    m, d = c.shape
    n = w.shape[1]
    tn = 1024
    return pl.pallas_call(
        _ada_kernel,
        grid=(n // tn,),
        in_specs=[pl.BlockSpec((m, d), lambda j: (0, 0)),
                  pl.BlockSpec((d, tn), lambda j: (0, j)),
                  pl.BlockSpec((1, tn), lambda j: (0, j))],
        out_specs=pl.BlockSpec((m, tn), lambda j: (0, j)),
        out_shape=jax.ShapeDtypeStruct((m, n), F32),
        compiler_params=_params(("arbitrary",)),
        name="ada_ln",
    )(c, w, b.reshape(1, n))


_C_KV = 0
_C_X = 6 * KV_DIM
_C_B = _C_X + CONV_DIM
_C_C = _C_B + CONV_DIM
_C_MA = _C_C + CONV_DIM
_C_MC = _C_MA + D_MODEL
_R_Q = 0
_R_VS = ATTN_DIM
_R_VW = _R_VS + KV_DIM
_R_G = _R_VW + KV_DIM
GATE_ROWS = 32
_R_END = _R_G + GATE_ROWS
TOK_TILE = 128


def _inproj_kernel(x_ref, shift_ref, scale_ref, g1_ref, w_ref, wt_ref, gq_ref, gks_ref, gkw_ref,
                   qt_ref, kc_ref, vc_ref, ks_ref, vs_ref, kw_ref, vw_ref, ksb_ref, kwb_ref,
                   vst_ref, vwt_ref, gnt_ref, u_ref, bg_ref, ma_ref, mc_ref):
    h = _rms_mod(x_ref[0], g1_ref[...], shift_ref[0], scale_ref[0]).astype(BF16)

    def proj(c0, n):
        return jnp.dot(h, w_ref[:, c0:c0 + n], preferred_element_type=F32)

    kc_ref[0] = proj(_C_KV, KV_DIM)
    vc_ref[0] = proj(_C_KV + KV_DIM, KV_DIM)
    ks = _head_rms(proj(_C_KV + 2 * KV_DIM, KV_DIM), gks_ref[...])
    ks_ref[0] = ks
    ksb_ref[0] = ks.astype(BF16)
    vs_ref[0] = proj(_C_KV + 3 * KV_DIM, KV_DIM)
    kw = _head_rms(proj(_C_KV + 4 * KV_DIM, KV_DIM), gkw_ref[...])
    kw_ref[0] = kw
    kwb_ref[0] = kw.astype(BF16)
    vw_ref[0] = proj(_C_KV + 5 * KV_DIM, KV_DIM)
    u_ref[0] = proj(_C_C, CONV_DIM) * proj(_C_X, CONV_DIM)
    bg_ref[0] = proj(_C_B, CONV_DIM)
    ma_ref[0] = _sigmoid(proj(_C_MA, D_MODEL))
    mc_ref[0] = _sigmoid(proj(_C_MC, D_MODEL))

    gq = gq_ref[...]
    for r in range(h.shape[0] // TOK_TILE):
        zt = _bdot_nt(wt_ref[...], h[r * TOK_TILE:(r + 1) * TOK_TILE])
        heads = []
        for hd in range(N_HEADS):
            z = zt[HEAD_DIM * hd:HEAD_DIM * (hd + 1)]
            ms = jnp.sum(z * z, axis=0, keepdims=True) * (1.0 / HEAD_DIM)
            heads.append(z * lax.rsqrt(ms + NORM_EPS) * gq[HEAD_DIM * hd:HEAD_DIM * (hd + 1)] * ATTN_SCALE)
        qt_ref[0, r] = jnp.concatenate(heads, axis=0).astype(BF16)
        vst_ref[0, r] = zt[_R_VS:_R_VS + KV_DIM].astype(BF16)
        vwt_ref[0, r] = zt[_R_VW:_R_VW + KV_DIM].astype(BF16)
        gnt_ref[0, r] = _sigmoid(zt[_R_G:_R_G + GATE_ROWS])


def _in_proj(x, shift, scale, g1, w_rows, w_feat, gq, gks, gkw, tm):
    b, s, d = x.shape
    tm = min(tm, s)
    assert s % tm == 0 and tm % TOK_TILE == 0
    r = shift.shape[1]
    rb = 1 if r == 1 else tm
    mod_map = (lambda i, j: (i, 0, 0)) if r == 1 else (lambda i, j: (i, j, 0))
    row = lambda n: pl.BlockSpec((1, tm, n), lambda i, j: (i, j, 0))
    feat = lambda n: pl.BlockSpec((1, tm // TOK_TILE, n, TOK_TILE), lambda i, j: (i, j, 0, 0))
    const = lambda a: pl.BlockSpec(a.shape, lambda i, j: (0, 0))
    rows_f32 = lambda n: jax.ShapeDtypeStruct((b, s, n), F32)
    rows_b16 = lambda n: jax.ShapeDtypeStruct((b, s, n), BF16)
    feat_sh = lambda n, dt: jax.ShapeDtypeStruct((b, s // TOK_TILE, n, TOK_TILE), dt)
    out_specs = ([feat(ATTN_DIM)] + [row(KV_DIM)] * 8 + [feat(KV_DIM), feat(KV_DIM), feat(GATE_ROWS)]
                 + [row(CONV_DIM), row(CONV_DIM), row(d), row(d)])
    out_shape = ([feat_sh(ATTN_DIM, BF16)] + [rows_f32(KV_DIM)] * 6 + [rows_b16(KV_DIM)] * 2
                 + [feat_sh(KV_DIM, BF16), feat_sh(KV_DIM, BF16), feat_sh(GATE_ROWS, F32)]
                 + [rows_f32(CONV_DIM), rows_f32(CONV_DIM), rows_f32(d), rows_f32(d)])
    return pl.pallas_call(
        _inproj_kernel,
        grid=(b, s // tm),
        in_specs=[row(d), pl.BlockSpec((1, rb, d), mod_map), pl.BlockSpec((1, rb, d), mod_map),
                  const(g1), const(w_rows), const(w_feat), const(gq), const(gks), const(gkw)],
        out_specs=out_specs,
        out_shape=out_shape,
        compiler_params=_params(("arbitrary", "arbitrary")),
        name="in_proj",
    )(x, shift, scale, g1, w_rows, w_feat, gq, gks, gkw)


def _compress_kernel(pt_ref, kpool, vpool, w1k_ref, w1v_ref, pe_ref, w2k_ref, w2v_ref, gk_ref,
                     kcmp_ref, vcmp_ref, kbuf, vbuf, sem, *, n_pages):
    b = pl.program_id(0)
    nb = pl.num_programs(0)
    slot = b % 2

    def page_copy(pool, buf, seq, p, sl, which):
        return pltpu.make_async_copy(pool.at[pt_ref[seq, p]], buf.at[sl, p], sem.at[which, sl])

    def fetch(seq, sl):
        for p in range(n_pages):
            page_copy(kpool, kbuf, seq, p, sl, 0).start()
            page_copy(vpool, vbuf, seq, p, sl, 1).start()

    @pl.when(b == 0)
    def _():
        fetch(0, 0)

    @pl.when(b + 1 < nb)
    def _():
        fetch(b + 1, 1 - slot)

    for p in range(n_pages):
        page_copy(kpool, kbuf, b, p, slot, 0).wait()
        page_copy(vpool, vbuf, b, p, slot, 1).wait()

    n_chunk = n_pages * CHUNKS_PER_PAGE
    half = N_KV_HEADS * CMP_HID

    def summarise(buf, w1_ref, w2_ref, pe0, pe1):
        a = jnp.zeros((n_chunk, 2 * half), F32)
        for s in range(0, CMP_STRIDE, 2):
            x = jnp.concatenate(
                [buf[slot, :, pl.ds(s + j, CHUNKS_PER_PAGE, stride=CMP_STRIDE), :].reshape(n_chunk, KV_DIM)
                 for j in range(2)], axis=1).astype(BF16)
            a = a + jnp.dot(x, w1_ref[s * KV_DIM:(s + 2) * KV_DIM, :], preferred_element_type=F32)
        pe_term = (jnp.dot(pe0.astype(BF16), w1_ref[:, :half], preferred_element_type=F32)
                   + jnp.dot(pe1.astype(BF16), w1_ref[:, half:], preferred_element_type=F32))
        nxt = pltpu.roll(a[:, half:], n_chunk - 1, 0)
        hid = a[:, :half] + nxt + pe_term[0:1]
        return jnp.dot(_gelu_tanh(hid).astype(BF16), w2_ref[...], preferred_element_type=F32)

    kcmp_ref[0] = _head_rms(summarise(kbuf, w1k_ref, w2k_ref, pe_ref[0], pe_ref[1]), gk_ref[...])
    vcmp_ref[0] = summarise(vbuf, w1v_ref, w2v_ref, pe_ref[2], pe_ref[3])


def _compress(page_table, kpool, vpool, cw):
    n_seq, n_pages = page_table.shape
    n_chunk = n_pages * CHUNKS_PER_PAGE
    kp, vp = kpool, vpool
    const = lambda a: pl.BlockSpec(a.shape, lambda i, pt: (0,) * a.ndim)
    out = pl.BlockSpec((1, n_chunk, KV_DIM), lambda i, pt: (i, 0, 0))
    grid_spec = pltpu.PrefetchScalarGridSpec(
        num_scalar_prefetch=1,
        grid=(n_seq,),
        in_specs=[pl.BlockSpec(memory_space=pl.ANY), pl.BlockSpec(memory_space=pl.ANY),
                  const(cw["w1k"]), const(cw["w1v"]), const(cw["pe"]),
                  const(cw["w2k"]), const(cw["w2v"]), const(cw["gk"])],
        out_specs=[out, out],
        scratch_shapes=[pltpu.VMEM((2, n_pages, PAGE_ROWS, KV_DIM), F32),
                        pltpu.VMEM((2, n_pages, PAGE_ROWS, KV_DIM), F32),
                        pltpu.SemaphoreType.DMA((2, 2))],
    )
    return pl.pallas_call(
        functools.partial(_compress_kernel, n_pages=n_pages),
        grid_spec=grid_spec,
        out_shape=[jax.ShapeDtypeStruct((n_seq, n_chunk, KV_DIM), F32)] * 2,
        compiler_params=_params(("arbitrary",)),
        name="compress",
    )(page_table, kp, vp, cw["w1k"], cw["w1v"], cw["pe"], cw["w2k"], cw["w2v"], cw["gk"])


def _overlap(n_cmp, n_slc):
    c = lax.broadcasted_iota(I32, (n_cmp, n_slc), 0) * CMP_STRIDE
    j = lax.broadcasted_iota(I32, (n_cmp, n_slc), 1) * SEL_BLOCK
    return ((c < j + SEL_BLOCK) & (c + CMP_LEN > j)).astype(BF16)


def _overlap_t(n_slc, n_cmp):
    j = lax.broadcasted_iota(I32, (n_slc, n_cmp), 0) * SEL_BLOCK
    c = lax.broadcasted_iota(I32, (n_slc, n_cmp), 1) * CMP_STRIDE
    return ((c < j + SEL_BLOCK) & (c + CMP_LEN > j)).astype(BF16)


def _nsa_prompt_kernel(qt_ref, gnt_ref, kcmp_ref, vcmp_ref, ksb_ref, vst_ref, kwb_ref, vwt_ref, o_ref,
                       qp_ref, selb_ref, rowb_ref, acc_ref, ot_ref, *, tq, tk, n_slc):
    i = pl.program_id(1)
    t0 = i * tq
    n_cmp = kcmp_ref.shape[1]
    n_top = min(SEL_TOP, n_slc)
    width = N_HEADS * tq
    gw = GROUP * tq
    slopes = [2.0 ** -(h + 1) for h in range(N_HEADS)]
    slope_row = jnp.concatenate([jnp.full((1, tq), s, F32) for s in slopes], axis=1)
    lane_t = t0 + lax.broadcasted_iota(I32, (1, tq), 1)
    gnt = jnp.concatenate([gnt_ref[0, r] for r in range(tq // TOK_TILE)], axis=1)
    gate = lambda h, j: gnt[3 * h + j:3 * h + j + 1]
    head = lambda a, h: a[:, h * tq:(h + 1) * tq]

    zero = jnp.zeros((HEAD_DIM, TOK_TILE), BF16)
    for h in range(N_HEADS):
        for r in range(tq // TOK_TILE):
            qh = qt_ref[0, r, HEAD_DIM * h:HEAD_DIM * (h + 1), :]
            c0 = h * tq + r * TOK_TILE
            qp_ref[:, c0:c0 + TOK_TILE] = jnp.concatenate([qh, zero] if h < GROUP else [zero, qh], axis=0)
    rowb_ref[...] = lax.broadcasted_iota(I32, (tk, width), 0).astype(F32) * slope_row

    kcb = kcmp_ref[0].astype(BF16)
    vct = vcmp_ref[0].T.astype(BF16)
    c_col = lax.broadcasted_iota(I32, (n_cmp, 1), 0) * CMP_STRIDE
    dist = lane_t.astype(F32) - (c_col.astype(F32) + (CMP_LEN - 1) / 2)
    vis = c_col + (CMP_LEN - 1) <= lane_t
    j_col = lax.broadcasted_iota(I32, (n_slc, 1), 0)
    cur = lane_t // SEL_BLOCK
    forced = (j_col == 0) | (j_col == cur) | (j_col == cur - 1)
    valid = j_col * SEL_BLOCK <= lane_t
    sub = lax.broadcasted_iota(I32, (8, 1), 0)
    sc_all = jnp.dot(kcb, qp_ref[...], preferred_element_type=F32)
    probs = []
    for h in range(N_HEADS):
        sc = jnp.where(vis, head(sc_all, h) - slopes[h] * dist, NEG)
        m = jnp.max(sc, axis=0, keepdims=True)
        e = jnp.where(vis, jnp.exp(sc - m), 0.0)
        probs.append(e / jnp.maximum(jnp.sum(e, axis=0, keepdims=True), 1e-30))
    o_cmp = jnp.dot(vct, jnp.concatenate(probs, axis=1).astype(BF16), preferred_element_type=F32)
    for h in range(N_HEADS):
        g = h // GROUP
        ot_ref[HEAD_DIM * h:HEAD_DIM * (h + 1)] = gate(h, 0) * head(o_cmp[HEAD_DIM * g:HEAD_DIM * (g + 1)], h)
    for g in range(N_KV_HEADS):
        psum = probs[GROUP * g]
        for r in range(1, GROUP):
            psum = psum + probs[GROUP * g + r]
        hi = psum.astype(BF16)
        lo = (psum - hi.astype(F32)).astype(BF16)
        ovt = _overlap_t(n_slc, n_cmp)
        imp = (jnp.dot(ovt, hi, preferred_element_type=F32)
               + jnp.dot(ovt, lo, preferred_element_type=F32))
        imp = jnp.where(valid, jnp.where(forced, FORCED_SCORE, imp), -1.0)
        groups = [imp[8 * v:8 * (v + 1)] for v in range(n_slc // 8)]
        rank = [jnp.zeros((8, tq), F32) for _ in groups]
        for b in range(n_slc):
            row = imp[b:b + 1]
            for v, gv in enumerate(groups):
                if v < b // 8:
                    ahead = row > gv
                elif v > b // 8:
                    ahead = row >= gv
                else:
                    ahead = (row > gv) | ((row == gv) & (sub > b % 8))
                rank[v] = rank[v] + jnp.where(ahead, 1.0, 0.0)
        rank = jnp.concatenate(rank, axis=0)
        selb_ref[g] = jnp.where((rank < n_top) & valid, 0.0, NEG)

    half = SEL_BLOCK

    def tile_step(kt, carry, k_ref, vt_ref, use_sel, causal, window):
        m_i, l_i = carry
        s0 = pl.multiple_of(kt * tk, tk)
        k = k_ref[0, pl.ds(s0, tk), :]
        n_sub = tk // TOK_TILE
        vt = jnp.concatenate([vt_ref[0, kt * n_sub + j] for j in range(n_sub)], axis=1)
        off = t0 - s0
        s = jnp.dot(k, qp_ref[...], preferred_element_type=F32) + rowb_ref[...]
        base = slope_row * (-off.astype(F32))
        if use_sel:
            blk = kt * (tk // SEL_BLOCK)
            rows = []
            for j in range(tk // SEL_BLOCK):
                sel = [selb_ref[g, pl.ds(blk + j, 1), :] for g in range(N_KV_HEADS)]
                rows.append(base + jnp.concatenate([sel[h // GROUP] for h in range(N_HEADS)], axis=1))
            s = jnp.concatenate([s[j * half:(j + 1) * half] + rows[j] for j in range(tk // SEL_BLOCK)], axis=0)
        else:
            s = s + base
        if causal or window:
            d_kq = (lax.broadcasted_iota(I32, (tk, width), 0)
                    - (lax.broadcasted_iota(I32, (tk, width), 1) & (tq - 1)))
            bad = None
            if causal:
                bad = d_kq > off
            if window:
                late = d_kq < off - WINDOW
                bad = late if bad is None else (bad | late)
            s = jnp.where(bad, NEG, s)
        m_new = jnp.maximum(m_i, jnp.max(s, axis=0, keepdims=True))
        alpha = jnp.exp(m_i - m_new)
        p = jnp.exp(s - m_new)
        l_new = alpha * l_i + jnp.sum(p, axis=0, keepdims=True)
        pb = p.astype(BF16)
        pv = jnp.concatenate(
            [jnp.dot(vt[HEAD_DIM * g:HEAD_DIM * (g + 1)], pb[:, g * gw:(g + 1) * gw], preferred_element_type=F32)
             for g in range(N_KV_HEADS)], axis=1)
        acc_ref[...] = alpha * acc_ref[...] + pv
        return m_new, l_new

    def branch(k_ref, vt_ref, use_sel, lo, mid, hi, gate_idx):
        acc_ref[...] = jnp.zeros_like(acc_ref)
        carry = (jnp.full((1, width), NEG, F32), jnp.zeros((1, width), F32))
        window = not use_sel
        carry = lax.fori_loop(lo, mid, lambda kt, c: tile_step(kt, c, k_ref, vt_ref, use_sel, False, window),
                              carry)
        carry = lax.fori_loop(mid, hi, lambda kt, c: tile_step(kt, c, k_ref, vt_ref, use_sel, True, window),
                              carry)
        out = acc_ref[...] / jnp.maximum(carry[1], 1e-30)
        for h in range(N_HEADS):
            rows = slice(HEAD_DIM * h, HEAD_DIM * (h + 1))
            ot_ref[rows] = ot_ref[rows] + gate(h, gate_idx) * head(out, h)

    first_now = t0 // tk
    end = (t0 + tq) // tk
    branch(ksb_ref, vst_ref, True, 0, first_now, end, 1)
    branch(kwb_ref, vwt_ref, False, jnp.maximum(first_now - WINDOW // tk, 0), first_now, end, 2)
    o_ref[0] = ot_ref[...].T


def _nsa_prompt(qt, gnt, kcmp, vcmp, ksb, vst, kwb, vwt, tq, tk):
    b, n_tile, _, _ = qt.shape
    s = n_tile * TOK_TILE
    tq = min(tq, s)
    tk = min(tk, tq)
    assert s % tq == 0 and tq % tk == 0 and tk % TOK_TILE == 0 and WINDOW % tk == 0 and tq & (tq - 1) == 0
    n_slc = -(-s // SEL_BLOCK)
    assert n_slc % 8 == 0
    n_cmp = kcmp.shape[1]
    nq = tq // TOK_TILE
    tile = lambda n: pl.BlockSpec((1, nq, n, TOK_TILE), lambda i, j: (i, j, 0, 0))
    seq = lambda n, w: pl.BlockSpec((1, n, w), lambda i, j: (i, 0, 0))
    feat = pl.BlockSpec((1, n_tile, KV_DIM, TOK_TILE), lambda i, j: (i, 0, 0, 0))
    return pl.pallas_call(
        functools.partial(_nsa_prompt_kernel, tq=tq, tk=tk, n_slc=n_slc),
        grid=(b, s // tq),
        in_specs=[tile(ATTN_DIM), tile(GATE_ROWS), seq(n_cmp, KV_DIM), seq(n_cmp, KV_DIM),
                  seq(s, KV_DIM), feat, seq(s, KV_DIM), feat],
        out_specs=pl.BlockSpec((1, tq, ATTN_DIM), lambda i, j: (i, j, 0)),
        out_shape=jax.ShapeDtypeStruct((b, s, ATTN_DIM), F32),
        scratch_shapes=[pltpu.VMEM((KV_DIM, N_HEADS * tq), BF16),
                        pltpu.VMEM((N_KV_HEADS, n_slc, tq), F32),
                        pltpu.VMEM((tk, N_HEADS * tq), F32),
                        pltpu.VMEM((HEAD_DIM, N_HEADS * tq), F32),
                        pltpu.VMEM((ATTN_DIM, tq), F32)],
        compiler_params=_params(("arbitrary", "arbitrary")),
        name="nsa_prompt",
    )(qt, gnt, kcmp, vcmp, ksb, vst, kwb, vwt)


def _query_rows(q):
    rows = []
    for h in range(N_HEADS):
        qh = q[:, HEAD_DIM * h:HEAD_DIM * (h + 1)]
        z = jnp.zeros_like(qh)
        rows.append(jnp.concatenate([qh, z] if h < GROUP else [z, qh], axis=1))
    return jnp.concatenate(rows, axis=0)


def _head_slopes():
    return jnp.concatenate([jnp.full((1, 1), 2.0 ** -(h + 1), F32) for h in range(N_HEADS)], axis=0)


def _per_group(rows_iota, v0, v1):
    return jnp.where(rows_iota < GROUP, v0, v1)


def _sample_select_kernel(q_ref, kcmp_ref, vcmp_ref, ocmp_ref, idx_ref, *, t_pos, n_slc, n_lane):
    qr = _query_rows(q_ref[0])
    n_cmp = kcmp_ref.shape[1]
    slope = _head_slopes()
    sc = _bdot_nt(qr, kcmp_ref[0])
    c_start = lax.broadcasted_iota(I32, (1, n_cmp), 1) * CMP_STRIDE
    dist = float(t_pos) - (c_start.astype(F32) + (CMP_LEN - 1) / 2)
    vis = c_start + (CMP_LEN - 1) <= t_pos
    sc = jnp.where(vis, sc - slope * dist, NEG)
    m = jnp.max(sc, axis=-1, keepdims=True)
    e = jnp.where(vis, jnp.exp(sc - m), 0.0)
    p = e / jnp.maximum(jnp.sum(e, axis=-1, keepdims=True), 1e-30)
    ocmp_ref[0] = _bdot(p, vcmp_ref[0])

    hrow = lax.broadcasted_iota(I32, (N_HEADS, 1), 0)
    psum = jnp.concatenate(
        [jnp.sum(jnp.where((hrow // GROUP) == g, p, 0.0), axis=0, keepdims=True) for g in range(N_KV_HEADS)]
        + [jnp.zeros((8 - N_KV_HEADS, n_cmp), F32)], axis=0)
    imp = _split_dot(psum, _overlap(n_cmp, n_lane))
    j_row = lax.broadcasted_iota(I32, (1, n_lane), 1)
    cur = t_pos // SEL_BLOCK
    forced = (j_row == 0) | (j_row == cur) | (j_row == cur - 1)
    valid = (j_row * SEL_BLOCK <= t_pos) & (j_row < n_slc)
    imp = jnp.where(valid, jnp.where(forced, FORCED_SCORE, imp), -1.0)
    n_top = min(SEL_TOP, n_slc)
    ii = lax.broadcasted_iota(I32, (n_lane, n_lane), 0)
    jj = lax.broadcasted_iota(I32, (n_lane, n_lane), 1)
    slot = lax.broadcasted_iota(I32, (1, SEL_TOP), 1)
    rows = []
    for g in range(N_KV_HEADS):
        by_lane = jnp.broadcast_to(imp[g:g + 1], (n_lane, n_lane))
        col = jnp.sum(jnp.where(ii == jj, by_lane, 0.0), axis=1, keepdims=True)
        by_row = jnp.broadcast_to(col, (n_lane, n_lane))
        beaten = (by_lane > by_row) | ((by_lane == by_row) & (jj < ii))
        rank_col = jnp.sum(beaten.astype(F32), axis=1, keepdims=True)
        hit = rank_col == slot.astype(F32)
        block = jnp.sum(jnp.where(hit, ii[:, :SEL_TOP].astype(F32), 0.0), axis=0, keepdims=True)
        rows.append(jnp.where(slot < n_top, block, float(n_lane - 1)).astype(I32))
    idx_ref[0] = jnp.concatenate(rows, axis=0)


def _sample_select(q, kcmp, vcmp, t_pos):
    db = q.shape[0]
    n_cmp = kcmp.shape[1]
    n_slc = -(-(t_pos + 1) // SEL_BLOCK)
    n_lane = -(-n_slc // 128) * 128
    return pl.pallas_call(
        functools.partial(_sample_select_kernel, t_pos=t_pos, n_slc=n_slc, n_lane=n_lane),
        grid=(db,),
        in_specs=[pl.BlockSpec((1, 1, ATTN_DIM), lambda i: (i, 0, 0)),
                  pl.BlockSpec((1, n_cmp, KV_DIM), lambda i: (i, 0, 0)),
                  pl.BlockSpec((1, n_cmp, KV_DIM), lambda i: (i, 0, 0))],
        out_specs=[pl.BlockSpec((1, N_HEADS, KV_DIM), lambda i: (i, 0, 0)),
                   pl.BlockSpec((1, N_KV_HEADS, SEL_TOP), lambda i: (i, 0, 0))],
        out_shape=[jax.ShapeDtypeStruct((db, N_HEADS, KV_DIM), F32),
                   jax.ShapeDtypeStruct((db, N_KV_HEADS, SEL_TOP), I32)],
        compiler_params=_params(("arbitrary",)),
        name="sample_select",
    )(q, kcmp, vcmp)


def _sample_attend_kernel(idx_s, pt_s, q_ref, idx_ref, ksn_ref, vsn_ref, kwb_ref, vwb_ref, kwn_ref, vwn_ref,
                          gn_ref, ocmp_ref, kpool, vpool, o_ref, kbuf, vbuf, sem, *, t_pos, n_past_blk):
    b = pl.program_id(0)
    nb = pl.num_programs(0)
    slot = b % 2
    n_sel = N_KV_HEADS * SEL_TOP
    blk_per_page = PAGE_ROWS // SEL_BLOCK

    def block_copy(pool, buf, seq, s, sl, which):
        blk = jnp.minimum(idx_s[seq * n_sel + s], n_past_blk - 1)
        page = pt_s[seq, blk // blk_per_page]
        off = pl.multiple_of((blk % blk_per_page) * SEL_BLOCK, SEL_BLOCK)
        return pltpu.make_async_copy(pool.at[page, pl.ds(off, SEL_BLOCK)], buf.at[sl, s], sem.at[which, sl])

    def fetch(seq, sl):
        for s in range(n_sel):
            block_copy(kpool, kbuf, seq, s, sl, 0).start()
            block_copy(vpool, vbuf, seq, s, sl, 1).start()

    @pl.when(b == 0)
    def _():
        fetch(0, 0)

    @pl.when(b + 1 < nb)
    def _():
        fetch(b + 1, 1 - slot)

    for s in range(n_sel):
        block_copy(kpool, kbuf, b, s, slot, 0).wait()
        block_copy(vpool, vbuf, b, s, slot, 1).wait()

    qr = _query_rows(q_ref[0])
    slope = _head_slopes()
    hrow = lax.broadcasted_iota(I32, (N_HEADS, 1), 0)

    def with_new_key(s, mask, v, s_new, new_ok, v_new):
        s = jnp.where(mask, s, NEG)
        s_new = jnp.where(new_ok, s_new, NEG)
        m = jnp.maximum(jnp.max(s, axis=-1, keepdims=True), s_new)
        e = jnp.where(mask, jnp.exp(s - m), 0.0)
        e_new = jnp.where(new_ok, jnp.exp(s_new - m), 0.0)
        l = jnp.sum(e, axis=-1, keepdims=True) + e_new
        return (_bdot(e, v) + e_new * v_new) / jnp.maximum(l, 1e-30)

    n_col = SEL_TOP * SEL_BLOCK
    k_all = kbuf[slot].reshape(n_sel * SEL_BLOCK, KV_DIM)
    v_all = vbuf[slot].reshape(n_sel * SEL_BLOCK, KV_DIM)
    idx8 = jnp.concatenate([idx_ref[0].astype(F32), jnp.zeros((8 - N_KV_HEADS, SEL_TOP), F32)], axis=0)
    expand = (lax.broadcasted_iota(I32, (SEL_TOP, n_col), 0)
              == lax.broadcasted_iota(I32, (SEL_TOP, n_col), 1) // SEL_BLOCK)
    blk8 = _bdot(idx8, expand.astype(BF16))
    blk = jnp.concatenate([blk8[g:g + 1] for g in range(N_KV_HEADS)], axis=1)
    col = lax.broadcasted_iota(I32, (1, N_KV_HEADS * n_col), 1)
    pos = blk * SEL_BLOCK + (col % SEL_BLOCK).astype(F32)
    own = (col // n_col) == (hrow // GROUP)
    mask = own & (blk < n_past_blk)
    s_sel = _bdot_nt(qr, k_all) - slope * (float(t_pos) - pos)
    cur = float(t_pos // SEL_BLOCK)
    has_new = [jnp.max(jnp.where(idx8[g:g + 1] == cur, 1.0, 0.0), axis=-1, keepdims=True)
               for g in range(N_KV_HEADS)]
    new_ok = _per_group(hrow, has_new[0], has_new[1]) > 0.5
    s_new = jnp.sum(qr * ksn_ref[0], axis=-1, keepdims=True)
    o_sel = with_new_key(s_sel, mask, v_all, s_new, new_ok, vsn_ref[0])

    w_buf = kwb_ref.shape[1]
    delta = w_buf - lax.broadcasted_iota(I32, (1, w_buf), 1)
    wmask = (delta <= WINDOW) & (t_pos - delta >= 0)
    s_win = _bdot_nt(qr, kwb_ref[0]) - slope * delta.astype(F32)
    s_wnew = jnp.sum(qr * kwn_ref[0], axis=-1, keepdims=True)
    o_win = with_new_key(s_win, jnp.broadcast_to(wmask, s_win.shape), vwb_ref[0], s_wnew, hrow >= 0, vwn_ref[0])

    lane = lax.broadcasted_iota(I32, (N_HEADS, GATE_PAD), 1)
    gn = jnp.broadcast_to(gn_ref[0], (N_HEADS, GATE_PAD))
    gate = lambda j: jnp.sum(jnp.where(lane == 3 * hrow + j, gn, 0.0), axis=-1, keepdims=True)
    o = gate(0) * ocmp_ref[0] + gate(1) * o_sel + gate(2) * o_win
    o_ref[0] = jnp.concatenate(
        [o[h:h + 1, HEAD_DIM * (h // GROUP):HEAD_DIM * (h // GROUP + 1)] for h in range(N_HEADS)], axis=1)


def _sample_attend(idx, page_table, q, ks_new, vs_new, kw_buf, vw_buf, kw_new, vw_new, gn, ocmp,
                   ks_pool, vs_pool, t_pos):
    db = q.shape[0]
    w_buf = kw_buf.shape[1]
    n_sel = N_KV_HEADS * SEL_TOP
    n_past_blk = page_table.shape[1] * (PAGE_ROWS // SEL_BLOCK)
    row = lambda n: pl.BlockSpec((1, 1, n), lambda i, a, b: (i, 0, 0))
    grid_spec = pltpu.PrefetchScalarGridSpec(
        num_scalar_prefetch=2,
        grid=(db,),
        in_specs=[row(ATTN_DIM),
                  pl.BlockSpec((1, N_KV_HEADS, SEL_TOP), lambda i, a, b: (i, 0, 0)),
                  row(KV_DIM), row(KV_DIM),
                  pl.BlockSpec((1, w_buf, KV_DIM), lambda i, a, b: (i, 0, 0)),
                  pl.BlockSpec((1, w_buf, KV_DIM), lambda i, a, b: (i, 0, 0)),
                  row(KV_DIM), row(KV_DIM), row(GATE_PAD),
                  pl.BlockSpec((1, N_HEADS, KV_DIM), lambda i, a, b: (i, 0, 0)),
                  pl.BlockSpec(memory_space=pl.ANY), pl.BlockSpec(memory_space=pl.ANY)],
        out_specs=row(ATTN_DIM),
        scratch_shapes=[pltpu.VMEM((2, n_sel, SEL_BLOCK, KV_DIM), F32),
                        pltpu.VMEM((2, n_sel, SEL_BLOCK, KV_DIM), F32),
                        pltpu.SemaphoreType.DMA((2, 2))],
    )
    return pl.pallas_call(
        functools.partial(_sample_attend_kernel, t_pos=t_pos, n_past_blk=n_past_blk),
        grid_spec=grid_spec,
        out_shape=jax.ShapeDtypeStruct((db, 1, ATTN_DIM), F32),
        compiler_params=_params(("arbitrary",)),
        name="sample_attend",
    )(idx.reshape(-1), page_table, q, idx, ks_new, vs_new, kw_buf, vw_buf, kw_new, vw_new, gn, ocmp,
      ks_pool, vs_pool)


def _merge_kernel(x_ref, oatt_ref, u_ref, hist_ref, bg_ref, ma_ref, mc_ref, cw_ref,
                  wao_ref, wco_ref, wo_ref, gate1_ref, shift2_ref, scale2_ref, g2_ref,
                  x1_ref, h2_ref, *, per_token_history):
    u = u_ref[0]
    tm = u.shape[0]
    if per_token_history:
        u2, u1 = hist_ref[0], hist_ref[1]
    else:
        prev = hist_ref[0]
        first = pl.program_id(1) == 0
        p1 = jnp.where(first, 0.0, prev[7:8])
        p2 = jnp.where(first, 0.0, prev[6:7])
        r = lax.broadcasted_iota(I32, (tm, 1), 0)
        u1 = jnp.where(r == 0, p1, pltpu.roll(u, 1, 0))
        u2 = jnp.where(r == 0, p2, jnp.where(r == 1, p1, pltpu.roll(u, 2, 0)))
    cw = cw_ref[...]
    conv = u2 * cw[0:1] + u1 * cw[1:2] + u * cw[2:3]
    y_conv = bg_ref[0] * conv
    mix = ma_ref[0] * _bdot(oatt_ref[0], wao_ref[...]) + mc_ref[0] * _bdot(y_conv, wco_ref[...])
    x1 = x_ref[0] + gate1_ref[0] * _bdot(mix, wo_ref[...])
    x1_ref[0] = x1
    h2_ref[0] = _rms_mod(x1, g2_ref[...], shift2_ref[0], scale2_ref[0])


def _merge(x, oatt, u, hist, bg, ma, mc, conv_w, wao, wco, wo, gate1, shift2, scale2, g2, tm):
    b, s, d = x.shape
    tm = min(tm, s)
    per_token = hist is not None
    r = gate1.shape[1]
    rb = 1 if r == 1 else tm
    mod_map = (lambda i, j: (i, 0, 0)) if r == 1 else (lambda i, j: (i, j, 0))
    row = lambda n: pl.BlockSpec((1, tm, n), lambda i, j: (i, j, 0))
    const = lambda a: pl.BlockSpec(a.shape, lambda i, j: (0, 0))
    mod = pl.BlockSpec((1, rb, d), mod_map)
    if per_token:
        assert b == 1
        hist_arr = hist
        hist_spec = pl.BlockSpec((2, tm, CONV_DIM), lambda i, j: (0, j, 0))
    else:
        hist_arr = u
        hist_spec = pl.BlockSpec((1, 8, CONV_DIM), lambda i, j: (i, jnp.maximum(j * (tm // 8) - 1, 0), 0))
    return pl.pallas_call(
        functools.partial(_merge_kernel, per_token_history=per_token),
        grid=(b, s // tm),
        in_specs=[row(d), row(ATTN_DIM), row(CONV_DIM), hist_spec, row(CONV_DIM), row(d), row(d),
                  const(conv_w), const(wao), const(wco), const(wo), mod, mod, mod, const(g2)],
        out_specs=[row(d), row(d)],
        out_shape=[jax.ShapeDtypeStruct((b, s, d), F32)] * 2,
        compiler_params=_params(("arbitrary", "arbitrary")),
        name="merge",
    )(x, oatt, u, hist_arr, bg, ma, mc, conv_w, wao, wco, wo, gate1, shift2, scale2, g2)


def _route_kernel(h_ref, rwt_ref, rb_ref, e_ref, w_ref, pos_ref, cnt_ref):
    @pl.when(pl.program_id(0) == 0)
    def _():
        cnt_ref[...] = jnp.zeros_like(cnt_ref)

    tm = h_ref.shape[0]
    per_group = N_EXPERTS // N_GROUPS
    aff = _sigmoid(_bdot_nt(rwt_ref[...], h_ref[...]))
    biased = aff + rb_ref[...]
    row = lax.broadcasted_iota(I32, (N_EXPERTS, tm), 0)
    lrow = lax.broadcasted_iota(I32, (per_group, tm), 0)
    ninf = -jnp.inf

    score = []
    for g in range(N_GROUPS):
        xg = biased[g * per_group:(g + 1) * per_group]
        m1 = jnp.max(xg, axis=0, keepdims=True)
        i1 = jnp.min(jnp.where(xg == m1, lrow, per_group), axis=0, keepdims=True)
        m2 = jnp.max(jnp.where(lrow == i1, ninf, xg), axis=0, keepdims=True)
        score.append(m1 + m2)
    cand = []
    for a in range(N_GROUPS):
        ahead = jnp.zeros((1, tm), F32)
        for b in range(N_GROUPS):
            if b != a:
                wins = (score[b] > score[a]) | ((score[b] == score[a]) & (b < a))
                ahead = ahead + wins.astype(F32)
        cand.append(jnp.where(ahead < TOPK_GROUPS, biased[a * per_group:(a + 1) * per_group], ninf))
    cand = jnp.concatenate(cand, axis=0)

    chosen = jnp.zeros((N_EXPERTS, tm), F32)
    e_rows, w_rows = [], []
    for _ in range(TOP_K):
        m = jnp.max(cand, axis=0, keepdims=True)
        idx = jnp.min(jnp.where(cand == m, row, N_EXPERTS), axis=0, keepdims=True)
        hit = row == idx
        e_rows.append(idx)
        w_rows.append(jnp.sum(jnp.where(hit, aff, 0.0), axis=0, keepdims=True))
        cand = jnp.where(hit, ninf, cand)
        chosen = chosen + hit.astype(F32)
    total = w_rows[0]
    for w in w_rows[1:]:
        total = total + w
    w_rows = [w / total * ROUTED_SCALE for w in w_rows]

    earlier = (lax.broadcasted_iota(I32, (tm, tm), 0) < lax.broadcasted_iota(I32, (tm, tm), 1)).astype(BF16)
    before = _bdot(chosen, earlier) + cnt_ref[...]
    pos_rows = [jnp.sum(jnp.where(row == e, before, 0.0), axis=0, keepdims=True) for e in e_rows]
    cnt_ref[...] += jnp.sum(chosen, axis=1, keepdims=True)

    e_ref[...] = jnp.concatenate(e_rows, axis=0)
    w_ref[...] = jnp.concatenate(w_rows, axis=0)
    pos_ref[...] = jnp.concatenate(pos_rows, axis=0).astype(I32)


def _route(h, rwt, rb, tm):
    n, d = h.shape
    tm = min(tm, n)
    slot = pl.BlockSpec((TOP_K, tm), lambda i: (0, i))
    return pl.pallas_call(
        _route_kernel,
        grid=(n // tm,),
        in_specs=[pl.BlockSpec((tm, d), lambda i: (i, 0)),
                  pl.BlockSpec((N_EXPERTS, d), lambda i: (0, 0)),
                  pl.BlockSpec((N_EXPERTS, 1), lambda i: (0, 0))],
        out_specs=[slot, slot, slot, pl.BlockSpec((N_EXPERTS, 1), lambda i: (0, 0))],
        out_shape=[jax.ShapeDtypeStruct((TOP_K, n), I32), jax.ShapeDtypeStruct((TOP_K, n), F32),
                   jax.ShapeDtypeStruct((TOP_K, n), I32), jax.ShapeDtypeStruct((N_EXPERTS, 1), F32)],
        compiler_params=_params(("arbitrary",)),
        name="moe_route",
    )(h, rwt, rb)


def _dest_kernel(e_ref, pos_ref, start_ref, d_ref):
    tm = e_ref.shape[1]
    row = lax.broadcasted_iota(I32, (N_EXPERTS, tm), 0)
    start = start_ref[...]
    rows = []
    for k in range(TOP_K):
        base = jnp.sum(jnp.where(row == e_ref[k:k + 1], start, 0.0), axis=0, keepdims=True)
        rows.append(base.astype(I32) + pos_ref[k:k + 1])
    d_ref[...] = jnp.concatenate(rows, axis=0)


def _dest(e_t, pos_t, start, tm):
    n = e_t.shape[1]
    tm = min(tm, n)
    slot = pl.BlockSpec((TOP_K, tm), lambda i: (0, i))
    return pl.pallas_call(
        _dest_kernel,
        grid=(n // tm,),
        in_specs=[slot, slot, pl.BlockSpec((N_EXPERTS, 1), lambda i: (0, 0))],
        out_specs=slot,
        out_shape=jax.ShapeDtypeStruct((TOP_K, n), I32),
        compiler_params=_params(("arbitrary",)),
        name="moe_dest",
    )(e_t, pos_t, start)


def _dispatch_kernel(dest_ref, h_ref, xs_in, xs_out, sem):
    del xs_in
    tm = h_ref.shape[0]

    def row_copy(n, d):
        return pltpu.make_async_copy(h_ref.at[pl.ds(n, 1)], xs_out.at[pl.ds(d, 1)], sem)

    def issue(n, c):
        for k in range(TOP_K):
            row_copy(n, dest_ref[n * TOP_K + k]).start()
        return c

    lax.fori_loop(0, tm, issue, 0)
    for k in range(TOP_K):
        pltpu.make_async_copy(h_ref, xs_out.at[pl.ds(0, tm)], sem).wait()


def _dispatch(dest_flat, h, n_rows, tm):
    n, d = h.shape
    tm = min(tm, n)
    xs = jnp.zeros((n_rows, d), F32)
    return pl.pallas_call(
        _dispatch_kernel,
        grid=(n // tm,),
        in_specs=[pl.BlockSpec((tm * TOP_K,), lambda i: (i,), memory_space=pltpu.SMEM),
                  pl.BlockSpec((tm, d), lambda i: (i, 0)),
                  pl.BlockSpec(memory_space=pl.ANY)],
        out_specs=pl.BlockSpec(memory_space=pl.ANY),
        out_shape=jax.ShapeDtypeStruct((n_rows, d), F32),
        scratch_shapes=[pltpu.SemaphoreType.DMA(())],
        input_output_aliases={2: 0},
        compiler_params=_params(("arbitrary",)),
        name="moe_dispatch",
    )(dest_flat, h, xs)


def _expert_kernel(be_ref, used_ref, x_ref, wg_ref, wu_ref, wd_ref, o_ref):
    i = pl.program_id(0)

    @pl.when(i < used_ref[0])
    def _():
        x = x_ref[...].astype(BF16)
        g = jnp.dot(x, wg_ref[0], preferred_element_type=F32)
        u = jnp.dot(x, wu_ref[0], preferred_element_type=F32)
        o_ref[...] = jnp.dot((_silu(g) * u).astype(BF16), wd_ref[0], preferred_element_type=F32)

    @pl.when(i >= used_ref[0])
    def _():
        o_ref[...] = jnp.zeros_like(o_ref)


def _experts(block_e, n_used, xs, wg, wu, wd):
    n_rows, d = xs.shape
    ff = wg.shape[2]
    grid_spec = pltpu.PrefetchScalarGridSpec(
        num_scalar_prefetch=2,
        grid=(n_rows // MOE_ROWS,),
        in_specs=[pl.BlockSpec((MOE_ROWS, d), lambda i, be, nu: (i, 0)),
                  pl.BlockSpec((1, d, ff), lambda i, be, nu: (be[i], 0, 0)),
                  pl.BlockSpec((1, d, ff), lambda i, be, nu: (be[i], 0, 0)),
                  pl.BlockSpec((1, ff, d), lambda i, be, nu: (be[i], 0, 0))],
        out_specs=pl.BlockSpec((MOE_ROWS, d), lambda i, be, nu: (i, 0)),
    )
    return pl.pallas_call(
        _expert_kernel,
        grid_spec=grid_spec,
        out_shape=jax.ShapeDtypeStruct((n_rows, d), F32),
        compiler_params=_params(("arbitrary",)),
        name="moe_experts",
    )(block_e, n_used, xs, wg, wu, wd)


def _combine_kernel(dest_ref, w_ref, x1_ref, h2_ref, gate2_ref, sg_ref, su_ref, sd_ref, ys_hbm,
                    o_ref, gbuf, sem):
    tm = x1_ref.shape[0]

    def row_copy(n, k, d):
        return pltpu.make_async_copy(ys_hbm.at[pl.ds(d, 1)], gbuf.at[k, pl.ds(n, 1)], sem)

    def issue(n, c):
        for k in range(TOP_K):
            row_copy(n, k, dest_ref[n * TOP_K + k]).start()
        return c

    lax.fori_loop(0, tm, issue, 0)
    h = h2_ref[...].astype(BF16)
    g = jnp.dot(h, sg_ref[...], preferred_element_type=F32)
    u = jnp.dot(h, su_ref[...], preferred_element_type=F32)
    y = jnp.dot((_silu(g) * u).astype(BF16), sd_ref[...], preferred_element_type=F32)
    for k in range(TOP_K):
        pltpu.make_async_copy(ys_hbm.at[pl.ds(0, tm)], gbuf.at[k], sem).wait()
    w = w_ref[...]
    for k in range(TOP_K):
        y = y + w[:, k:k + 1] * gbuf[k]
    o_ref[...] = x1_ref[...] + gate2_ref[0] * y


def _combine(dest_flat, w, x1, h2, gate2, tokens_per_gate, sg, su, sd, ys, tm):
    n, d = x1.shape
    tm = min(tm, n)
    row = pl.BlockSpec((tm, d), lambda i: (i, 0))
    const = lambda a: pl.BlockSpec(a.shape, lambda i: (0, 0))
    if gate2.shape[1] == 1:
        gate_spec = pl.BlockSpec((1, 1, d), lambda i: ((i * tm) // tokens_per_gate, 0, 0))
    else:
        gate_spec = pl.BlockSpec((1, tm, d), lambda i: (0, i, 0))
    return pl.pallas_call(
        _combine_kernel,
        grid=(n // tm,),
        in_specs=[pl.BlockSpec((tm * TOP_K,), lambda i: (i,), memory_space=pltpu.SMEM),
                  pl.BlockSpec((tm, TOP_K), lambda i: (i, 0)),
                  row, row, gate_spec, const(sg), const(su), const(sd),
                  pl.BlockSpec(memory_space=pl.ANY)],
        out_specs=row,
        out_shape=jax.ShapeDtypeStruct((n, d), F32),
        scratch_shapes=[pltpu.VMEM((TOP_K, tm, d), F32), pltpu.SemaphoreType.DMA(())],
        compiler_params=_params(("arbitrary",)),
        name="moe_combine",
    )(dest_flat, w, x1, h2, gate2, sg, su, sd, ys)


def _moe(h2, x1, gate2, tokens_per_gate, mw):
    n = h2.shape[0]
    e_t, w_t, pos_t, counts = _route(h2, mw["rwt"], mw["rb"], 256)
    counts = counts.reshape(-1).astype(I32)
    padded = (counts + MOE_ROWS - 1) // MOE_ROWS * MOE_ROWS
    pad_end = jnp.cumsum(padded)
    n_blocks = n * TOP_K // MOE_ROWS + N_EXPERTS
    start = (pad_end - padded).astype(F32).reshape(N_EXPERTS, 1)
    dest = _dest(e_t, pos_t, start, 512).T.reshape(-1)
    block_e = jnp.minimum(jnp.searchsorted(pad_end, jnp.arange(n_blocks, dtype=I32) * MOE_ROWS, side="right"),
                          N_EXPERTS - 1).astype(I32)
    n_used = (pad_end[-1:] // MOE_ROWS).astype(I32)
    xs = _dispatch(dest, h2, n_blocks * MOE_ROWS, 256)
    ys = _experts(block_e, n_used, xs, mw["wg"], mw["wu"], mw["wd"])
    return _combine(dest, w_t.T, x1, h2, gate2, tokens_per_gate, mw["sg"], mw["su"], mw["sd"], ys, 128)


def _pack_w_in(w_in):
    n_gate = N_HEADS * 3
    kv0 = ATTN_DIM
    g0 = kv0 + 6 * KV_DIM
    w_rows = jnp.concatenate([w_in[:, kv0:g0], w_in[:, g0 + n_gate:]], axis=1).astype(BF16)
    gate = jnp.pad(w_in[:, g0:g0 + n_gate], ((0, 0), (0, GATE_ROWS - n_gate)))
    v_sel = w_in[:, kv0 + 3 * KV_DIM:kv0 + 4 * KV_DIM]
    v_win = w_in[:, kv0 + 5 * KV_DIM:kv0 + 6 * KV_DIM]
    w_feat = jnp.concatenate([w_in[:, :ATTN_DIM], v_sel, v_win, gate], axis=1).T.astype(BF16)
    return w_rows, w_feat


def _compress_weights(cmp_pe, cmp_w1, cmp_w2, g_kcmp):
    eye = jnp.eye(N_KV_HEADS, dtype=F32)
    sub = CMP_LEN // CMP_STRIDE

    def w1_cat(w1):
        w = w1.reshape(sub, CMP_STRIDE, HEAD_DIM, CMP_HID)
        mats = [jnp.einsum("sdh,gk->sgdkh", w[j], eye).reshape(CMP_STRIDE * KV_DIM, N_KV_HEADS * CMP_HID)
                for j in range(sub)]
        return jnp.concatenate(mats, axis=1).astype(BF16)

    def w2_blk(w2):
        return jnp.einsum("hd,gk->ghkd", w2, eye).reshape(N_KV_HEADS * CMP_HID, KV_DIM).astype(BF16)

    def pe_rows(pe):
        p = pe.reshape(sub, CMP_STRIDE, 1, HEAD_DIM)
        p = jnp.broadcast_to(p, (sub, CMP_STRIDE, N_KV_HEADS, HEAD_DIM)).reshape(sub, 1, CMP_STRIDE * KV_DIM)
        return jnp.broadcast_to(p, (sub, 8, CMP_STRIDE * KV_DIM))

    assert sub == 2
    return {
        "w1k": w1_cat(cmp_w1[0]), "w1v": w1_cat(cmp_w1[1]),
        "w2k": w2_blk(cmp_w2[0]), "w2v": w2_blk(cmp_w2[1]),
        "pe": jnp.concatenate([pe_rows(cmp_pe[0]), pe_rows(cmp_pe[1])], axis=0),
        "gk": jnp.tile(g_kcmp, N_KV_HEADS).reshape(1, KV_DIM),
    }


def _layer(l, x_p, x_s, c_all, caches, page_table, w):
    ckc, cvc, cks, cvs, ckw, cvw, sconv = caches
    b, s, d = x_p.shape
    db, ds = x_s.shape[:2]
    assert ds == 1, "one new token per sample sequence"
    n_pool = ckc.shape[1]
    t_pos = page_table.shape[1] * PAGE_ROWS
    hist = CONV_WIDTH - 1

    ada = _ada(c_all, w["w_ada"][l], w["b_ada"][l])
    mods = [ada[:, k * d:(k + 1) * d] for k in range(6)]
    mod_p = [m[:b, None] for m in mods]
    mod_s = [m[b:][None] for m in mods]

    qkg = w["qk_norm_g"][l]
    gq = jnp.tile(qkg[0], N_HEADS).reshape(-1, 1)
    gks = jnp.tile(qkg[2], N_KV_HEADS).reshape(1, -1)
    gkw = jnp.tile(qkg[3], N_KV_HEADS).reshape(1, -1)
    g1 = w["norm1_g"][l].reshape(1, d)
    g2 = w["norm2_g"][l].reshape(1, d)
    w_rows, w_feat = _pack_w_in(w["w_in"][l])
    cw = _compress_weights(w["cmp_pe"][l], w["cmp_w1"][l], w["cmp_w2"][l], qkg[1])
    wao, wco, wo = (w[k][l].astype(BF16) for k in ("w_attn_out", "w_conv_out", "w_o"))
    mw = {"rwt": w["router_w"][l].T.astype(BF16), "rb": w["router_b"][l].reshape(-1, 1),
          "wg": w["exp_w_gate"][l].astype(BF16), "wu": w["exp_w_up"][l].astype(BF16),
          "wd": w["exp_w_down"][l].astype(BF16), "sg": w["shared_w_gate"][l].astype(BF16),
          "su": w["shared_w_up"][l].astype(BF16), "sd": w["shared_w_down"][l].astype(BF16)}
    pages = lambda a: a.reshape(-1, PAGE_ROWS, KV_DIM)

    (qt, kc, vc, ks, vs, kw, vw, ksb, kwb, vst, vwt, gnt, u, bg, ma, mc) = _in_proj(
        x_p, mod_p[0], mod_p[1], g1, w_rows, w_feat, gq, gks, gkw, 256)
    own_pages = jnp.arange(b * s // PAGE_ROWS, dtype=I32).reshape(b, s // PAGE_ROWS)
    kcmp, vcmp = _compress(own_pages, pages(kc), pages(vc), cw)
    o_att = _nsa_prompt(qt, gnt, kcmp, vcmp, ksb, vst, kwb, vwt, 256, 256)
    x1, h2 = _merge(x_p, o_att, u, None, bg, ma, mc, w["conv_w"][l], wao, wco, wo,
                    mod_p[2], mod_p[3], mod_p[4], g2, 256)
    y_p = _moe(h2.reshape(b * s, d), x1.reshape(b * s, d), mod_p[5], s, mw).reshape(b, s, d)
    keep = min(WINDOW, s)
    heads = lambda a: a.reshape(a.shape[0], a.shape[1], N_KV_HEADS, HEAD_DIM)
    p_state = (heads(kc), heads(vc), heads(ks), heads(vs), heads(kw[:, -keep:]), heads(vw[:, -keep:]),
               u[:, -hist:])

    (qt, kc, vc, ks, vs, kw, vw, _, _, _, _, gnt, u, bg, ma, mc) = _in_proj(
        x_s.reshape(1, db, d), mod_s[0], mod_s[1], g1, w_rows, w_feat, gq, gks, gkw, 128)
    per_seq = lambda a: a.reshape(db, 1, a.shape[-1])
    token_major = lambda a: jnp.swapaxes(a[0], 1, 2).reshape(db, a.shape[2]).astype(F32)
    q = token_major(qt)
    gn = jnp.pad(token_major(gnt), ((0, 0), (0, GATE_PAD - GATE_ROWS)))
    kcmp, vcmp = _compress(page_table, pages(ckc[l]), pages(cvc[l]), cw)
    o_cmp, idx = _sample_select(per_seq(q), kcmp, vcmp, t_pos)
    w_buf = ckw.shape[2]
    o_att = _sample_attend(idx, page_table, per_seq(q), per_seq(ks), per_seq(vs),
                           ckw[l].reshape(db, w_buf, KV_DIM), cvw[l].reshape(db, w_buf, KV_DIM),
                           per_seq(kw), per_seq(vw), per_seq(gn), o_cmp, pages(cks[l]), pages(cvs[l]), t_pos)
    x1, h2 = _merge(x_s.reshape(1, db, d), o_att.reshape(1, db, ATTN_DIM), u, jnp.swapaxes(sconv[l], 0, 1),
                    bg, ma, mc, w["conv_w"][l], wao, wco, wo, mod_s[2], mod_s[3], mod_s[4], g2, 128)
    y_s = _moe(h2.reshape(db, d), x1.reshape(db, d), mod_s[5], 1, mw).reshape(db, 1, d)
    new_row = lambda a: a.reshape(db, 1, N_KV_HEADS, HEAD_DIM)
    s_state = (new_row(kc), new_row(vc), new_row(ks), new_row(vs),
               jnp.concatenate([ckw[l], new_row(kw)], axis=1)[:, -w_buf:],
               jnp.concatenate([cvw[l], new_row(vw)], axis=1)[:, -w_buf:],
               jnp.concatenate([sconv[l], u.reshape(db, 1, CONV_DIM)], axis=1)[:, -hist:])
    return y_p, y_s, p_state + s_state


def kernel(x_prompt, x_sample, cache_k_cmp, cache_v_cmp, cache_k_sel, cache_v_sel, cache_k_win, cache_v_win,
           state_conv, page_table, c_prompt, c_sample, w_ada, b_ada, norm1_g, norm2_g, w_in, qk_norm_g,
           cmp_pe, cmp_w1, cmp_w2, conv_w, w_attn_out, w_conv_out, w_o, router_w, router_b,
           exp_w_gate, exp_w_up, exp_w_down, shared_w_gate, shared_w_up, shared_w_down):
    w = dict(w_ada=w_ada, b_ada=b_ada, norm1_g=norm1_g, norm2_g=norm2_g, w_in=w_in, qk_norm_g=qk_norm_g,
             cmp_pe=cmp_pe, cmp_w1=cmp_w1, cmp_w2=cmp_w2, conv_w=conv_w, w_attn_out=w_attn_out,
             w_conv_out=w_conv_out, w_o=w_o, router_w=router_w, router_b=router_b, exp_w_gate=exp_w_gate,
             exp_w_up=exp_w_up, exp_w_down=exp_w_down, shared_w_gate=shared_w_gate, shared_w_up=shared_w_up,
             shared_w_down=shared_w_down)
    caches = (cache_k_cmp, cache_v_cmp, cache_k_sel, cache_v_sel, cache_k_win, cache_v_win, state_conv)
    c_all = jnp.concatenate([c_prompt, c_sample], axis=0)
    x_p, x_s = x_prompt, x_sample
    states = []
    for l in range(w_ada.shape[0]):
        x_p, x_s, st = _layer(l, x_p, x_s, c_all, caches, page_table.astype(I32), w)
        states.append(st)
    return (x_p, x_s) + tuple(jnp.stack(s) for s in zip(*states))
```

```python
import functools

import jax
import jax.numpy as jnp
from jax import lax
from jax.experimental import pallas as pl
from jax.experimental.pallas import tpu as pltpu

F32 = jnp.float32
BF16 = jnp.bfloat16
I32 = jnp.int32

D_MODEL = 1024
N_HEADS = 8
HEAD_DIM = 64
N_KV_HEADS = 2
GROUP = N_HEADS // N_KV_HEADS
ATTN_DIM = N_HEADS * HEAD_DIM
KV_DIM = N_KV_HEADS * HEAD_DIM
ATTN_SCALE = HEAD_DIM ** -0.5
CMP_LEN = 32
CMP_STRIDE = 16
CMP_HID = 4 * HEAD_DIM
SEL_BLOCK = 64
SEL_TOP = 16
FORCED_SCORE = 1e6
WINDOW = 512
CONV_DIM = D_MODEL // 2
CONV_WIDTH = 3
N_EXPERTS = 256
TOP_K = 8
N_GROUPS = 8
TOPK_GROUPS = 4
EXPERT_FF = D_MODEL // 4
ROUTED_SCALE = 2.5
NORM_EPS = 1e-6
PAGE_ROWS = 128
CHUNKS_PER_PAGE = PAGE_ROWS // CMP_STRIDE
NEG = -1e30
GATE_PAD = 128
MOE_ROWS = 256
VMEM_LIMIT = 56 * 1024 * 1024


def _params(sem):
    return pltpu.CompilerParams(dimension_semantics=sem, vmem_limit_bytes=VMEM_LIMIT)


def _bdot(a, b):
    return jnp.dot(a.astype(BF16), b.astype(BF16), preferred_element_type=F32)


def _bdot_nt(a, b):
    return lax.dot_general(a.astype(BF16), b.astype(BF16), (((1,), (1,)), ((), ())),
                           preferred_element_type=F32)


def _split_dot(a, b_bf16, nt=False):
    hi = a.astype(BF16)
    lo = (a - hi.astype(F32)).astype(BF16)
    f = _bdot_nt if nt else _bdot
    return f(hi, b_bf16) + f(lo, b_bf16)


def _sigmoid(x):
    return 1.0 / (1.0 + jnp.exp(-x))


def _silu(x):
    return x * _sigmoid(x)


def _gelu_tanh(x):
    return 0.5 * x * (1.0 + jnp.tanh(0.7978845608028654 * (x + 0.044715 * (x * x * x))))


def _head_rms(z, gain):
    r = lax.broadcasted_iota(I32, (128, 128), 0) // HEAD_DIM
    c = lax.broadcasted_iota(I32, (128, 128), 1) // HEAD_DIM
    seg = (r == c).astype(BF16)
    x2 = z * z
    parts = [_split_dot(x2[:, j:j + 128], seg) for j in range(0, z.shape[1], 128)]
    ss = parts[0] if len(parts) == 1 else jnp.concatenate(parts, axis=1)
    return z * lax.rsqrt(ss * (1.0 / HEAD_DIM) + NORM_EPS) * gain


def _rms_mod(x, g, shift, scale):
    y = x * lax.rsqrt(jnp.mean(x * x, axis=-1, keepdims=True) + NORM_EPS) * g
    return y * (1.0 + scale) + shift


def _ada_kernel(c_ref, w_ref, b_ref, o_ref):
    o_ref[...] = _bdot(_silu(c_ref[...]), w_ref[...]) + b_ref[...]


def _ada(c, w, b):
    m, d = c.shape
    n = w.shape[1]
    tn = 1024
    return pl.pallas_call(
        _ada_kernel,
        grid=(n // tn,),
        in_specs=[pl.BlockSpec((m, d), lambda j: (0, 0)),
                  pl.BlockSpec((d, tn), lambda j: (0, j)),
                  pl.BlockSpec((1, tn), lambda j: (0, j))],
        out_specs=pl.BlockSpec((m, tn), lambda j: (0, j)),
        out_shape=jax.ShapeDtypeStruct((m, n), F32),
        compiler_params=_params(("arbitrary",)),
        name="ada_ln",
    )(c, w, b.reshape(1, n))


_C_KV = 0
_C_X = 6 * KV_DIM
_C_B = _C_X + CONV_DIM
_C_C = _C_B + CONV_DIM
_C_MA = _C_C + CONV_DIM
_C_MC = _C_MA + D_MODEL
_R_Q = 0
_R_VS = ATTN_DIM
_R_VW = _R_VS + KV_DIM
_R_G = _R_VW + KV_DIM
GATE_ROWS = 32
_R_END = _R_G + GATE_ROWS
TOK_TILE = 128


def _inproj_kernel(x_ref, shift_ref, scale_ref, g1_ref, w_ref, wt_ref, gq_ref, gks_ref, gkw_ref,
                   qt_ref, kc_ref, vc_ref, ks_ref, vs_ref, kw_ref, vw_ref, ksb_ref, kwb_ref,
                   vst_ref, vwt_ref, gnt_ref, u_ref, bg_ref, ma_ref, mc_ref):
    h = _rms_mod(x_ref[0], g1_ref[...], shift_ref[0], scale_ref[0]).astype(BF16)

    def proj(c0, n):
        return jnp.dot(h, w_ref[:, c0:c0 + n], preferred_element_type=F32)

    kc_ref[0] = proj(_C_KV, KV_DIM)
    vc_ref[0] = proj(_C_KV + KV_DIM, KV_DIM)
    ks = _head_rms(proj(_C_KV + 2 * KV_DIM, KV_DIM), gks_ref[...])
    ks_ref[0] = ks
    ksb_ref[0] = ks.astype(BF16)
    vs_ref[0] = proj(_C_KV + 3 * KV_DIM, KV_DIM)
    kw = _head_rms(proj(_C_KV + 4 * KV_DIM, KV_DIM), gkw_ref[...])
    kw_ref[0] = kw
    kwb_ref[0] = kw.astype(BF16)
    vw_ref[0] = proj(_C_KV + 5 * KV_DIM, KV_DIM)
    u_ref[0] = proj(_C_C, CONV_DIM) * proj(_C_X, CONV_DIM)
    bg_ref[0] = proj(_C_B, CONV_DIM)
    ma_ref[0] = _sigmoid(proj(_C_MA, D_MODEL))
    mc_ref[0] = _sigmoid(proj(_C_MC, D_MODEL))

    gq = gq_ref[...]
    for r in range(h.shape[0] // TOK_TILE):
        zt = _bdot_nt(wt_ref[...], h[r * TOK_TILE:(r + 1) * TOK_TILE])
        heads = []
        for hd in range(N_HEADS):
            z = zt[HEAD_DIM * hd:HEAD_DIM * (hd + 1)]
            ms = jnp.sum(z * z, axis=0, keepdims=True) * (1.0 / HEAD_DIM)
            heads.append(z * lax.rsqrt(ms + NORM_EPS) * gq[HEAD_DIM * hd:HEAD_DIM * (hd + 1)] * ATTN_SCALE)
        qt_ref[0, r] = jnp.concatenate(heads, axis=0).astype(BF16)
        vst_ref[0, r] = zt[_R_VS:_R_VS + KV_DIM].astype(BF16)
        vwt_ref[0, r] = zt[_R_VW:_R_VW + KV_DIM].astype(BF16)
        gnt_ref[0, r] = _sigmoid(zt[_R_G:_R_G + GATE_ROWS])


def _in_proj(x, shift, scale, g1, w_rows, w_feat, gq, gks, gkw, tm):
    b, s, d = x.shape
    tm = min(tm, s)
    assert s % tm == 0 and tm % TOK_TILE == 0
    r = shift.shape[1]
    rb = 1 if r == 1 else tm
    mod_map = (lambda i, j: (i, 0, 0)) if r == 1 else (lambda i, j: (i, j, 0))
    row = lambda n: pl.BlockSpec((1, tm, n), lambda i, j: (i, j, 0))
    feat = lambda n: pl.BlockSpec((1, tm // TOK_TILE, n, TOK_TILE), lambda i, j: (i, j, 0, 0))
    const = lambda a: pl.BlockSpec(a.shape, lambda i, j: (0, 0))
    rows_f32 = lambda n: jax.ShapeDtypeStruct((b, s, n), F32)
    rows_b16 = lambda n: jax.ShapeDtypeStruct((b, s, n), BF16)
    feat_sh = lambda n, dt: jax.ShapeDtypeStruct((b, s // TOK_TILE, n, TOK_TILE), dt)
    out_specs = ([feat(ATTN_DIM)] + [row(KV_DIM)] * 8 + [feat(KV_DIM), feat(KV_DIM), feat(GATE_ROWS)]
                 + [row(CONV_DIM), row(CONV_DIM), row(d), row(d)])
    out_shape = ([feat_sh(ATTN_DIM, BF16)] + [rows_f32(KV_DIM)] * 6 + [rows_b16(KV_DIM)] * 2
                 + [feat_sh(KV_DIM, BF16), feat_sh(KV_DIM, BF16), feat_sh(GATE_ROWS, F32)]
                 + [rows_f32(CONV_DIM), rows_f32(CONV_DIM), rows_f32(d), rows_f32(d)])
    return pl.pallas_call(
        _inproj_kernel,
        grid=(b, s // tm),
        in_specs=[row(d), pl.BlockSpec((1, rb, d), mod_map), pl.BlockSpec((1, rb, d), mod_map),
                  const(g1), const(w_rows), const(w_feat), const(gq), const(gks), const(gkw)],
        out_specs=out_specs,
        out_shape=out_shape,
        compiler_params=_params(("arbitrary", "arbitrary")),
        name="in_proj",
    )(x, shift, scale, g1, w_rows, w_feat, gq, gks, gkw)


def _compress_kernel(pt_ref, kpool, vpool, w1k_ref, w1v_ref, pe_ref, w2k_ref, w2v_ref, gk_ref,
                     kcmp_ref, vcmp_ref, kbuf, vbuf, sem, *, n_pages):
    b = pl.program_id(0)
    nb = pl.num_programs(0)
    slot = b % 2

    def page_copy(pool, buf, seq, p, sl, which):
        return pltpu.make_async_copy(pool.at[pt_ref[seq, p]], buf.at[sl, p], sem.at[which, sl])

    def fetch(seq, sl):
        for p in range(n_pages):
            page_copy(kpool, kbuf, seq, p, sl, 0).start()
            page_copy(vpool, vbuf, seq, p, sl, 1).start()

    @pl.when(b == 0)
    def _():
        fetch(0, 0)

    @pl.when(b + 1 < nb)
    def _():
        fetch(b + 1, 1 - slot)

    for p in range(n_pages):
        page_copy(kpool, kbuf, b, p, slot, 0).wait()
        page_copy(vpool, vbuf, b, p, slot, 1).wait()

    n_chunk = n_pages * CHUNKS_PER_PAGE
    half = N_KV_HEADS * CMP_HID

    def summarise(buf, w1_ref, w2_ref, pe0, pe1):
        a = jnp.zeros((n_chunk, 2 * half), F32)
        for s in range(0, CMP_STRIDE, 2):
            x = jnp.concatenate(
                [buf[slot, :, pl.ds(s + j, CHUNKS_PER_PAGE, stride=CMP_STRIDE), :].reshape(n_chunk, KV_DIM)
                 for j in range(2)], axis=1).astype(BF16)
            a = a + jnp.dot(x, w1_ref[s * KV_DIM:(s + 2) * KV_DIM, :], preferred_element_type=F32)
        pe_term = (jnp.dot(pe0.astype(BF16), w1_ref[:, :half], preferred_element_type=F32)
                   + jnp.dot(pe1.astype(BF16), w1_ref[:, half:], preferred_element_type=F32))
        nxt = pltpu.roll(a[:, half:], n_chunk - 1, 0)
        hid = a[:, :half] + nxt + pe_term[0:1]
        return jnp.dot(_gelu_tanh(hid).astype(BF16), w2_ref[...], preferred_element_type=F32)

    kcmp_ref[0] = _head_rms(summarise(kbuf, w1k_ref, w2k_ref, pe_ref[0], pe_ref[1]), gk_ref[...])
    vcmp_ref[0] = summarise(vbuf, w1v_ref, w2v_ref, pe_ref[2], pe_ref[3])


def _compress(page_table, kpool, vpool, cw):
    n_seq, n_pages = page_table.shape
    n_chunk = n_pages * CHUNKS_PER_PAGE
    kp, vp = kpool, vpool
    const = lambda a: pl.BlockSpec(a.shape, lambda i, pt: (0,) * a.ndim)
    out = pl.BlockSpec((1, n_chunk, KV_DIM), lambda i, pt: (i, 0, 0))
    grid_spec = pltpu.PrefetchScalarGridSpec(
        num_scalar_prefetch=1,
        grid=(n_seq,),
        in_specs=[pl.BlockSpec(memory_space=pl.ANY), pl.BlockSpec(memory_space=pl.ANY),
                  const(cw["w1k"]), const(cw["w1v"]), const(cw["pe"]),
                  const(cw["w2k"]), const(cw["w2v"]), const(cw["gk"])],
        out_specs=[out, out],
        scratch_shapes=[pltpu.VMEM((2, n_pages, PAGE_ROWS, KV_DIM), F32),
                        pltpu.VMEM((2, n_pages, PAGE_ROWS, KV_DIM), F32),
                        pltpu.SemaphoreType.DMA((2, 2))],
    )
    return pl.pallas_call(
        functools.partial(_compress_kernel, n_pages=n_pages),
        grid_spec=grid_spec,
        out_shape=[jax.ShapeDtypeStruct((n_seq, n_chunk, KV_DIM), F32)] * 2,
        compiler_params=_params(("arbitrary",)),
        name="compress",
    )(page_table, kp, vp, cw["w1k"], cw["w1v"], cw["pe"], cw["w2k"], cw["w2v"], cw["gk"])


def _overlap(n_cmp, n_slc):
    c = lax.broadcasted_iota(I32, (n_cmp, n_slc), 0) * CMP_STRIDE
    j = lax.broadcasted_iota(I32, (n_cmp, n_slc), 1) * SEL_BLOCK
    return ((c < j + SEL_BLOCK) & (c + CMP_LEN > j)).astype(BF16)


def _overlap_t(n_slc, n_cmp):
    j = lax.broadcasted_iota(I32, (n_slc, n_cmp), 0) * SEL_BLOCK
    c = lax.broadcasted_iota(I32, (n_slc, n_cmp), 1) * CMP_STRIDE
    return ((c < j + SEL_BLOCK) & (c + CMP_LEN > j)).astype(BF16)


def _nsa_prompt_kernel(qt_ref, gnt_ref, kcmp_ref, vcmp_ref, ksb_ref, vst_ref, kwb_ref, vwt_ref, o_ref,
                       qp_ref, selb_ref, rowb_ref, acc_ref, ot_ref, *, tq, tk, n_slc):
    i = pl.program_id(1)
    t0 = i * tq
    n_cmp = kcmp_ref.shape[1]
    n_top = min(SEL_TOP, n_slc)
    width = N_HEADS * tq
    gw = GROUP * tq
    slopes = [2.0 ** -(h + 1) for h in range(N_HEADS)]
    slope_row = jnp.concatenate([jnp.full((1, tq), s, F32) for s in slopes], axis=1)
    lane_t = t0 + lax.broadcasted_iota(I32, (1, tq), 1)
    gnt = jnp.concatenate([gnt_ref[0, r] for r in range(tq // TOK_TILE)], axis=1)
    gate = lambda h, j: gnt[3 * h + j:3 * h + j + 1]
    head = lambda a, h: a[:, h * tq:(h + 1) * tq]

    zero = jnp.zeros((HEAD_DIM, TOK_TILE), BF16)
    for h in range(N_HEADS):
        for r in range(tq // TOK_TILE):
            qh = qt_ref[0, r, HEAD_DIM * h:HEAD_DIM * (h + 1), :]
            c0 = h * tq + r * TOK_TILE
            qp_ref[:, c0:c0 + TOK_TILE] = jnp.concatenate([qh, zero] if h < GROUP else [zero, qh], axis=0)
    rowb_ref[...] = lax.broadcasted_iota(I32, (tk, width), 0).astype(F32) * slope_row

    kcb = kcmp_ref[0].astype(BF16)
    vct = vcmp_ref[0].T.astype(BF16)
    c_col = lax.broadcasted_iota(I32, (n_cmp, 1), 0) * CMP_STRIDE
    dist = lane_t.astype(F32) - (c_col.astype(F32) + (CMP_LEN - 1) / 2)
    vis = c_col + (CMP_LEN - 1) <= lane_t
    j_col = lax.broadcasted_iota(I32, (n_slc, 1), 0)
    cur = lane_t // SEL_BLOCK
    forced = (j_col == 0) | (j_col == cur) | (j_col == cur - 1)
    valid = j_col * SEL_BLOCK <= lane_t
    sub = lax.broadcasted_iota(I32, (8, 1), 0)
    sc_all = jnp.dot(kcb, qp_ref[...], preferred_element_type=F32)
    probs = []
    for h in range(N_HEADS):
        sc = jnp.where(vis, head(sc_all, h) - slopes[h] * dist, NEG)
        m = jnp.max(sc, axis=0, keepdims=True)
        e = jnp.where(vis, jnp.exp(sc - m), 0.0)
        probs.append(e / jnp.maximum(jnp.sum(e, axis=0, keepdims=True), 1e-30))
    o_cmp = jnp.dot(vct, jnp.concatenate(probs, axis=1).astype(BF16), preferred_element_type=F32)
    for h in range(N_HEADS):
        g = h // GROUP
        ot_ref[HEAD_DIM * h:HEAD_DIM * (h + 1)] = gate(h, 0) * head(o_cmp[HEAD_DIM * g:HEAD_DIM * (g + 1)], h)
    for g in range(N_KV_HEADS):
        psum = probs[GROUP * g]
        for r in range(1, GROUP):
            psum = psum + probs[GROUP * g + r]
        hi = psum.astype(BF16)
        lo = (psum - hi.astype(F32)).astype(BF16)
        ovt = _overlap_t(n_slc, n_cmp)
        imp = (jnp.dot(ovt, hi, preferred_element_type=F32)
               + jnp.dot(ovt, lo, preferred_element_type=F32))
        imp = jnp.where(valid, jnp.where(forced, FORCED_SCORE, imp), -1.0)
        groups = [imp[8 * v:8 * (v + 1)] for v in range(n_slc // 8)]
        rank = [jnp.zeros((8, tq), F32) for _ in groups]
        for b in range(n_slc):
            row = imp[b:b + 1]
            for v, gv in enumerate(groups):
                if v < b // 8:
                    ahead = row > gv
                elif v > b // 8:
                    ahead = row >= gv
                else:
                    ahead = (row > gv) | ((row == gv) & (sub > b % 8))
                rank[v] = rank[v] + jnp.where(ahead, 1.0, 0.0)
        rank = jnp.concatenate(rank, axis=0)
        selb_ref[g] = jnp.where((rank < n_top) & valid, 0.0, NEG)

    half = SEL_BLOCK

    def tile_step(kt, carry, k_ref, vt_ref, use_sel, causal, window):
        m_i, l_i = carry
        s0 = pl.multiple_of(kt * tk, tk)
        k = k_ref[0, pl.ds(s0, tk), :]
        n_sub = tk // TOK_TILE
        vt = jnp.concatenate([vt_ref[0, kt * n_sub + j] for j in range(n_sub)], axis=1)
        off = t0 - s0
        s = jnp.dot(k, qp_ref[...], preferred_element_type=F32) + rowb_ref[...]
        base = slope_row * (-off.astype(F32))
        if use_sel:
            blk = kt * (tk // SEL_BLOCK)
            rows = []
            for j in range(tk // SEL_BLOCK):
                sel = [selb_ref[g, pl.ds(blk + j, 1), :] for g in range(N_KV_HEADS)]
                rows.append(base + jnp.concatenate([sel[h // GROUP] for h in range(N_HEADS)], axis=1))
            s = jnp.concatenate([s[j * half:(j + 1) * half] + rows[j] for j in range(tk // SEL_BLOCK)], axis=0)
        else:
            s = s + base
        if causal or window:
            d_kq = (lax.broadcasted_iota(I32, (tk, width), 0)
                    - (lax.broadcasted_iota(I32, (tk, width), 1) & (tq - 1)))
            bad = None
            if causal:
                bad = d_kq > off
            if window:
                late = d_kq < off - WINDOW
                bad = late if bad is None else (bad | late)
            s = jnp.where(bad, NEG, s)
        m_new = jnp.maximum(m_i, jnp.max(s, axis=0, keepdims=True))
        alpha = jnp.exp(m_i - m_new)
        p = jnp.exp(s - m_new)
        l_new = alpha * l_i + jnp.sum(p, axis=0, keepdims=True)
        pb = p.astype(BF16)
        pv = jnp.concatenate(
            [jnp.dot(vt[HEAD_DIM * g:HEAD_DIM * (g + 1)], pb[:, g * gw:(g + 1) * gw], preferred_element_type=F32)
             for g in range(N_KV_HEADS)], axis=1)
        acc_ref[...] = alpha * acc_ref[...] + pv
        return m_new, l_new

    def branch(k_ref, vt_ref, use_sel, lo, mid, hi, gate_idx):
        acc_ref[...] = jnp.zeros_like(acc_ref)
        carry = (jnp.full((1, width), NEG, F32), jnp.zeros((1, width), F32))
        window = not use_sel
        carry = lax.fori_loop(lo, mid, lambda kt, c: tile_step(kt, c, k_ref, vt_ref, use_sel, False, window),
                              carry)
        carry = lax.fori_loop(mid, hi, lambda kt, c: tile_step(kt, c, k_ref, vt_ref, use_sel, True, window),
                              carry)
        out = acc_ref[...] / jnp.maximum(carry[1], 1e-30)
        for h in range(N_HEADS):
            rows = slice(HEAD_DIM * h, HEAD_DIM * (h + 1))
            ot_ref[rows] = ot_ref[rows] + gate(h, gate_idx) * head(out, h)

    first_now = t0 // tk
    end = (t0 + tq) // tk
    branch(ksb_ref, vst_ref, True, 0, first_now, end, 1)
    branch(kwb_ref, vwt_ref, False, jnp.maximum(first_now - WINDOW // tk, 0), first_now, end, 2)
    o_ref[0] = ot_ref[...].T


def _nsa_prompt(qt, gnt, kcmp, vcmp, ksb, vst, kwb, vwt, tq, tk):
    b, n_tile, _, _ = qt.shape
    s = n_tile * TOK_TILE
    tq = min(tq, s)
    tk = min(tk, tq)
    assert s % tq == 0 and tq % tk == 0 and tk % TOK_TILE == 0 and WINDOW % tk == 0 and tq & (tq - 1) == 0
    n_slc = -(-s // SEL_BLOCK)
    assert n_slc % 8 == 0
    n_cmp = kcmp.shape[1]
    nq = tq // TOK_TILE
    tile = lambda n: pl.BlockSpec((1, nq, n, TOK_TILE), lambda i, j: (i, j, 0, 0))
    seq = lambda n, w: pl.BlockSpec((1, n, w), lambda i, j: (i, 0, 0))
    feat = pl.BlockSpec((1, n_tile, KV_DIM, TOK_TILE), lambda i, j: (i, 0, 0, 0))
    return pl.pallas_call(
        functools.partial(_nsa_prompt_kernel, tq=tq, tk=tk, n_slc=n_slc),
        grid=(b, s // tq),
        in_specs=[tile(ATTN_DIM), tile(GATE_ROWS), seq(n_cmp, KV_DIM), seq(n_cmp, KV_DIM),
                  seq(s, KV_DIM), feat, seq(s, KV_DIM), feat],
        out_specs=pl.BlockSpec((1, tq, ATTN_DIM), lambda i, j: (i, j, 0)),
        out_shape=jax.ShapeDtypeStruct((b, s, ATTN_DIM), F32),
        scratch_shapes=[pltpu.VMEM((KV_DIM, N_HEADS * tq), BF16),
                        pltpu.VMEM((N_KV_HEADS, n_slc, tq), F32),
                        pltpu.VMEM((tk, N_HEADS * tq), F32),
                        pltpu.VMEM((HEAD_DIM, N_HEADS * tq), F32),
                        pltpu.VMEM((ATTN_DIM, tq), F32)],
        compiler_params=_params(("arbitrary", "arbitrary")),
        name="nsa_prompt",
    )(qt, gnt, kcmp, vcmp, ksb, vst, kwb, vwt)


def _query_rows(q):
    rows = []
    for h in range(N_HEADS):
        qh = q[:, HEAD_DIM * h:HEAD_DIM * (h + 1)]
        z = jnp.zeros_like(qh)
        rows.append(jnp.concatenate([qh, z] if h < GROUP else [z, qh], axis=1))
    return jnp.concatenate(rows, axis=0)


def _head_slopes():
    return jnp.concatenate([jnp.full((1, 1), 2.0 ** -(h + 1), F32) for h in range(N_HEADS)], axis=0)


def _sample_select_kernel(q_ref, kcmp_ref, vcmp_ref, ocmp_ref, idx_ref, *, t_pos, n_slc, n_lane):
    qr = _query_rows(q_ref[0])
    n_cmp = kcmp_ref.shape[1]
    slope = _head_slopes()
    sc = _bdot_nt(qr, kcmp_ref[0])
    c_start = lax.broadcasted_iota(I32, (1, n_cmp), 1) * CMP_STRIDE
    dist = float(t_pos) - (c_start.astype(F32) + (CMP_LEN - 1) / 2)
    vis = c_start + (CMP_LEN - 1) <= t_pos
    sc = jnp.where(vis, sc - slope * dist, NEG)
    m = jnp.max(sc, axis=-1, keepdims=True)
    e = jnp.where(vis, jnp.exp(sc - m), 0.0)
    p = e / jnp.maximum(jnp.sum(e, axis=-1, keepdims=True), 1e-30)
    ocmp_ref[0] = _bdot(p, vcmp_ref[0])

    hrow = lax.broadcasted_iota(I32, (N_HEADS, 1), 0)
    psum = jnp.concatenate(
        [jnp.sum(jnp.where((hrow // GROUP) == g, p, 0.0), axis=0, keepdims=True) for g in range(N_KV_HEADS)]
        + [jnp.zeros((8 - N_KV_HEADS, n_cmp), F32)], axis=0)
    imp = _split_dot(psum, _overlap(n_cmp, n_lane))
    j_row = lax.broadcasted_iota(I32, (1, n_lane), 1)
    cur = t_pos // SEL_BLOCK
    forced = (j_row == 0) | (j_row == cur) | (j_row == cur - 1)
    valid = (j_row * SEL_BLOCK <= t_pos) & (j_row < n_slc)
    imp = jnp.where(valid, jnp.where(forced, FORCED_SCORE, imp), -1.0)
    n_top = min(SEL_TOP, n_slc)
    ii = lax.broadcasted_iota(I32, (n_lane, n_lane), 0)
    jj = lax.broadcasted_iota(I32, (n_lane, n_lane), 1)
    slot = lax.broadcasted_iota(I32, (1, SEL_TOP), 1)
    rows = []
    for g in range(N_KV_HEADS):
        by_lane = jnp.broadcast_to(imp[g:g + 1], (n_lane, n_lane))
        col = jnp.sum(jnp.where(ii == jj, by_lane, 0.0), axis=1, keepdims=True)
        by_row = jnp.broadcast_to(col, (n_lane, n_lane))
        beaten = (by_lane > by_row) | ((by_lane == by_row) & (jj < ii))
        rank_col = jnp.sum(beaten.astype(F32), axis=1, keepdims=True)
        hit = rank_col == slot.astype(F32)
        block = jnp.sum(jnp.where(hit, ii[:, :SEL_TOP].astype(F32), 0.0), axis=0, keepdims=True)
        rows.append(jnp.where(slot < n_top, block, float(n_lane - 1)).astype(I32))
    idx_ref[0] = jnp.concatenate(rows, axis=0)


def _sample_select(q, kcmp, vcmp, t_pos):
    db = q.shape[0]
    n_cmp = kcmp.shape[1]
    n_slc = -(-(t_pos + 1) // SEL_BLOCK)
    n_lane = -(-n_slc // 128) * 128
    return pl.pallas_call(
        functools.partial(_sample_select_kernel, t_pos=t_pos, n_slc=n_slc, n_lane=n_lane),
        grid=(db,),
        in_specs=[pl.BlockSpec((1, 1, ATTN_DIM), lambda i: (i, 0, 0)),
                  pl.BlockSpec((1, n_cmp, KV_DIM), lambda i: (i, 0, 0)),
                  pl.BlockSpec((1, n_cmp, KV_DIM), lambda i: (i, 0, 0))],
        out_specs=[pl.BlockSpec((1, N_HEADS, KV_DIM), lambda i: (i, 0, 0)),
                   pl.BlockSpec((1, N_KV_HEADS, SEL_TOP), lambda i: (i, 0, 0))],
        out_shape=[jax.ShapeDtypeStruct((db, N_HEADS, KV_DIM), F32),
                   jax.ShapeDtypeStruct((db, N_KV_HEADS, SEL_TOP), I32)],
        compiler_params=_params(("arbitrary",)),
        name="sample_select",
    )(q, kcmp, vcmp)


def _sample_attend_kernel(idx_s, pt_s, q_ref, idx_ref, ksn_ref, vsn_ref, kwb_ref, vwb_ref, kwn_ref, vwn_ref,
                          gn_ref, ocmp_ref, kpool, vpool, o_ref, kbuf, vbuf, sem, *, t_pos, n_past_blk):
    b = pl.program_id(0)
    nb = pl.num_programs(0)
    slot = b % 2
    n_sel = N_KV_HEADS * SEL_TOP
    blk_per_page = PAGE_ROWS // SEL_BLOCK

    def page_copy(pool, buf, seq, s, sl, which):
        blk = jnp.minimum(idx_s[seq * n_sel + s], n_past_blk - 1)
        page = pt_s[seq, blk // blk_per_page]
        return pltpu.make_async_copy(pool.at[page, s // SEL_TOP], buf.at[sl, s], sem.at[which, sl])

    def fetch(seq, sl):
        for s in range(n_sel):
            page_copy(kpool, kbuf, seq, s, sl, 0).start()
            page_copy(vpool, vbuf, seq, s, sl, 1).start()

    @pl.when(b == 0)
    def _():
        fetch(0, 0)

    @pl.when(b + 1 < nb)
    def _():
        fetch(b + 1, 1 - slot)

    for s in range(n_sel):
        page_copy(kpool, kbuf, b, s, slot, 0).wait()
        page_copy(vpool, vbuf, b, s, slot, 1).wait()

    q = q_ref[0]
    qr = _query_rows(q)
    slope = _head_slopes()
    hrow = lax.broadcasted_iota(I32, (N_HEADS, 1), 0)
    own_half = lambda x: jnp.where(hrow < GROUP, x[:, :HEAD_DIM], x[:, HEAD_DIM:])

    n_col = SEL_TOP * PAGE_ROWS
    idx8 = jnp.concatenate([idx_ref[0].astype(F32), jnp.zeros((8 - N_KV_HEADS, SEL_TOP), F32)], axis=0)
    expand = (lax.broadcasted_iota(I32, (SEL_TOP, n_col), 0)
              == lax.broadcasted_iota(I32, (SEL_TOP, n_col), 1) // PAGE_ROWS)
    blk8 = _bdot(idx8, expand.astype(BF16)).astype(I32)
    col = lax.broadcasted_iota(I32, (1, n_col), 1)
    row_in_page = col % PAGE_ROWS
    cur = float(t_pos // SEL_BLOCK)
    zero_rows = jnp.zeros((8 - GROUP, HEAD_DIM), F32)
    o_sel = []
    for g in range(N_KV_HEADS):
        kt = jnp.concatenate([kbuf[slot, g * SEL_TOP + s] for s in range(SEL_TOP)], axis=1)
        vt = jnp.concatenate([vbuf[slot, g * SEL_TOP + s] for s in range(SEL_TOP)], axis=1)
        qg = jnp.concatenate([q[:, HEAD_DIM * h:HEAD_DIM * (h + 1)] for h in range(GROUP * g, GROUP * (g + 1))]
                             + [zero_rows], axis=0)
        sl_g = jnp.concatenate([jnp.full((1, 1), 2.0 ** -(GROUP * g + r + 1), F32) for r in range(GROUP)]
                               + [jnp.ones((8 - GROUP, 1), F32)], axis=0)
        blk = blk8[g:g + 1]
        mask = ((row_in_page // SEL_BLOCK) == (blk % blk_per_page)) & (blk < n_past_blk)
        pos = (blk * SEL_BLOCK + row_in_page % SEL_BLOCK).astype(F32)
        s = jnp.where(mask, _bdot(qg, kt) - sl_g * (float(t_pos) - pos), NEG)
        new_ok = jnp.max(jnp.where(idx8[g:g + 1] == cur, 1.0, 0.0), axis=-1, keepdims=True) > 0.5
        s_new = jnp.sum(qg * ksn_ref[0][:, HEAD_DIM * g:HEAD_DIM * (g + 1)], axis=-1, keepdims=True)
        s_new = jnp.where(new_ok, s_new, NEG)
        m = jnp.maximum(jnp.max(s, axis=-1, keepdims=True), s_new)
        e = jnp.where(mask, jnp.exp(s - m), 0.0)
        e_new = jnp.where(new_ok, jnp.exp(s_new - m), 0.0)
        l = jnp.sum(e, axis=-1, keepdims=True) + e_new
        pv = _bdot_nt(e, vt) + e_new * vsn_ref[0][:, HEAD_DIM * g:HEAD_DIM * (g + 1)]
        o_sel.append((pv / jnp.maximum(l, 1e-30))[:GROUP])
    o_sel = jnp.concatenate(o_sel, axis=0)

    w_buf = kwb_ref.shape[1]
    delta = w_buf - lax.broadcasted_iota(I32, (1, w_buf), 1)
    wmask = jnp.broadcast_to((delta <= WINDOW) & (t_pos - delta >= 0), (N_HEADS, w_buf))
    s_win = jnp.where(wmask, _bdot_nt(qr, kwb_ref[0]) - slope * delta.astype(F32), NEG)
    s_wnew = jnp.sum(qr * kwn_ref[0], axis=-1, keepdims=True)
    m = jnp.maximum(jnp.max(s_win, axis=-1, keepdims=True), s_wnew)
    e = jnp.where(wmask, jnp.exp(s_win - m), 0.0)
    e_new = jnp.exp(s_wnew - m)
    l = jnp.sum(e, axis=-1, keepdims=True) + e_new
    o_win = (_bdot(e, vwb_ref[0]) + e_new * vwn_ref[0]) / jnp.maximum(l, 1e-30)

    lane = lax.broadcasted_iota(I32, (N_HEADS, GATE_PAD), 1)
    gn = jnp.broadcast_to(gn_ref[0], (N_HEADS, GATE_PAD))
    gate = lambda j: jnp.sum(jnp.where(lane == 3 * hrow + j, gn, 0.0), axis=-1, keepdims=True)
    o = gate(0) * own_half(ocmp_ref[0]) + gate(1) * o_sel + gate(2) * own_half(o_win)
    o_ref[0] = jnp.concatenate([o[h:h + 1] for h in range(N_HEADS)], axis=1)


def _sample_attend(idx, page_table, q, ks_new, vs_new, kw_buf, vw_buf, kw_new, vw_new, gn, ocmp,
                   ks_pool, vs_pool, t_pos):
    db = q.shape[0]
    w_buf = kw_buf.shape[1]
    n_sel = N_KV_HEADS * SEL_TOP
    n_past_blk = page_table.shape[1] * (PAGE_ROWS // SEL_BLOCK)
    row = lambda n: pl.BlockSpec((1, 1, n), lambda i, a, b: (i, 0, 0))
    grid_spec = pltpu.PrefetchScalarGridSpec(
        num_scalar_prefetch=2,
        grid=(db,),
        in_specs=[row(ATTN_DIM),
                  pl.BlockSpec((1, N_KV_HEADS, SEL_TOP), lambda i, a, b: (i, 0, 0)),
                  row(KV_DIM), row(KV_DIM),
                  pl.BlockSpec((1, w_buf, KV_DIM), lambda i, a, b: (i, 0, 0)),
                  pl.BlockSpec((1, w_buf, KV_DIM), lambda i, a, b: (i, 0, 0)),
                  row(KV_DIM), row(KV_DIM), row(GATE_PAD),
                  pl.BlockSpec((1, N_HEADS, KV_DIM), lambda i, a, b: (i, 0, 0)),
                  pl.BlockSpec(memory_space=pl.ANY), pl.BlockSpec(memory_space=pl.ANY)],
        out_specs=row(ATTN_DIM),
        scratch_shapes=[pltpu.VMEM((2, n_sel, HEAD_DIM, PAGE_ROWS), F32),
                        pltpu.VMEM((2, n_sel, HEAD_DIM, PAGE_ROWS), F32),
                        pltpu.SemaphoreType.DMA((2, 2))],
    )
    return pl.pallas_call(
        functools.partial(_sample_attend_kernel, t_pos=t_pos, n_past_blk=n_past_blk),
        grid_spec=grid_spec,
        out_shape=jax.ShapeDtypeStruct((db, 1, ATTN_DIM), F32),
        compiler_params=_params(("arbitrary",)),
        name="sample_attend",
    )(idx.reshape(-1), page_table, q, idx, ks_new, vs_new, kw_buf, vw_buf, kw_new, vw_new, gn, ocmp,
      ks_pool, vs_pool)


def _merge_kernel(x_ref, oatt_ref, u_ref, hist_ref, bg_ref, ma_ref, mc_ref, cw_ref,
                  wao_ref, wco_ref, wo_ref, gate1_ref, shift2_ref, scale2_ref, g2_ref,
                  x1_ref, h2_ref, *, per_token_history):
    u = u_ref[0]
    tm = u.shape[0]
    if per_token_history:
        u2, u1 = hist_ref[0], hist_ref[1]
    else:
        prev = hist_ref[0]
        first = pl.program_id(1) == 0
        p1 = jnp.where(first, 0.0, prev[7:8])
        p2 = jnp.where(first, 0.0, prev[6:7])
        r = lax.broadcasted_iota(I32, (tm, 1), 0)
        u1 = jnp.where(r == 0, p1, pltpu.roll(u, 1, 0))
        u2 = jnp.where(r == 0, p2, jnp.where(r == 1, p1, pltpu.roll(u, 2, 0)))
    cw = cw_ref[...]
    conv = u2 * cw[0:1] + u1 * cw[1:2] + u * cw[2:3]
    y_conv = bg_ref[0] * conv
    mix = ma_ref[0] * _bdot(oatt_ref[0], wao_ref[...]) + mc_ref[0] * _bdot(y_conv, wco_ref[...])
    x1 = x_ref[0] + gate1_ref[0] * _bdot(mix, wo_ref[...])
    x1_ref[0] = x1
    h2_ref[0] = _rms_mod(x1, g2_ref[...], shift2_ref[0], scale2_ref[0])


def _merge(x, oatt, u, hist, bg, ma, mc, conv_w, wao, wco, wo, gate1, shift2, scale2, g2, tm):
    b, s, d = x.shape
    tm = min(tm, s)
    per_token = hist is not None
    r = gate1.shape[1]
    rb = 1 if r == 1 else tm
    mod_map = (lambda i, j: (i, 0, 0)) if r == 1 else (lambda i, j: (i, j, 0))
    row = lambda n: pl.BlockSpec((1, tm, n), lambda i, j: (i, j, 0))
    const = lambda a: pl.BlockSpec(a.shape, lambda i, j: (0, 0))
    mod = pl.BlockSpec((1, rb, d), mod_map)
    if per_token:
        assert b == 1
        hist_arr = hist
        hist_spec = pl.BlockSpec((2, tm, CONV_DIM), lambda i, j: (0, j, 0))
    else:
        hist_arr = u
        hist_spec = pl.BlockSpec((1, 8, CONV_DIM), lambda i, j: (i, jnp.maximum(j * (tm // 8) - 1, 0), 0))
    return pl.pallas_call(
        functools.partial(_merge_kernel, per_token_history=per_token),
        grid=(b, s // tm),
        in_specs=[row(d), row(ATTN_DIM), row(CONV_DIM), hist_spec, row(CONV_DIM), row(d), row(d),
                  const(conv_w), const(wao), const(wco), const(wo), mod, mod, mod, const(g2)],
        out_specs=[row(d), row(d)],
        out_shape=[jax.ShapeDtypeStruct((b, s, d), F32)] * 2,
        compiler_params=_params(("arbitrary", "arbitrary")),
        name="merge",
    )(x, oatt, u, hist_arr, bg, ma, mc, conv_w, wao, wco, wo, gate1, shift2, scale2, g2)


def _route_kernel(h_ref, rwt_ref, rb_ref, e_ref, w_ref, pos_ref, cnt_ref):
    @pl.when(pl.program_id(0) == 0)
    def _():
        cnt_ref[...] = jnp.zeros_like(cnt_ref)

    tm = h_ref.shape[0]
    per_group = N_EXPERTS // N_GROUPS
    aff = _sigmoid(_bdot_nt(rwt_ref[...], h_ref[...]))
    biased = aff + rb_ref[...]
    row = lax.broadcasted_iota(I32, (N_EXPERTS, tm), 0)
    lrow = lax.broadcasted_iota(I32, (per_group, tm), 0)
    ninf = -jnp.inf

    score = []
    for g in range(N_GROUPS):
        xg = biased[g * per_group:(g + 1) * per_group]
        m1 = jnp.max(xg, axis=0, keepdims=True)
        i1 = jnp.min(jnp.where(xg == m1, lrow, per_group), axis=0, keepdims=True)
        m2 = jnp.max(jnp.where(lrow == i1, ninf, xg), axis=0, keepdims=True)
        score.append(m1 + m2)
    cand = []
    for a in range(N_GROUPS):
        ahead = jnp.zeros((1, tm), F32)
        for b in range(N_GROUPS):
            if b != a:
                wins = (score[b] > score[a]) | ((score[b] == score[a]) & (b < a))
                ahead = ahead + wins.astype(F32)
        cand.append(jnp.where(ahead < TOPK_GROUPS, biased[a * per_group:(a + 1) * per_group], ninf))
    cand = jnp.concatenate(cand, axis=0)

    chosen = jnp.zeros((N_EXPERTS, tm), F32)
    e_rows, w_rows = [], []
    for _ in range(TOP_K):
        m = jnp.max(cand, axis=0, keepdims=True)
        idx = jnp.min(jnp.where(cand == m, row, N_EXPERTS), axis=0, keepdims=True)
        hit = row == idx
        e_rows.append(idx)
        w_rows.append(jnp.sum(jnp.where(hit, aff, 0.0), axis=0, keepdims=True))
        cand = jnp.where(hit, ninf, cand)
        chosen = chosen + hit.astype(F32)
    total = w_rows[0]
    for w in w_rows[1:]:
        total = total + w
    w_rows = [w / total * ROUTED_SCALE for w in w_rows]

    earlier = (lax.broadcasted_iota(I32, (tm, tm), 0) < lax.broadcasted_iota(I32, (tm, tm), 1)).astype(BF16)
    before = _bdot(chosen, earlier) + cnt_ref[...]
    pos_rows = [jnp.sum(jnp.where(row == e, before, 0.0), axis=0, keepdims=True) for e in e_rows]
    cnt_ref[...] += jnp.sum(chosen, axis=1, keepdims=True)

    e_ref[...] = jnp.concatenate(e_rows, axis=0)
    w_ref[...] = jnp.concatenate(w_rows, axis=0)
    pos_ref[...] = jnp.concatenate(pos_rows, axis=0).astype(I32)


def _route(h, rwt, rb, tm):
    n, d = h.shape
    tm = min(tm, n)
    slot = pl.BlockSpec((TOP_K, tm), lambda i: (0, i))
    return pl.pallas_call(
        _route_kernel,
        grid=(n // tm,),
        in_specs=[pl.BlockSpec((tm, d), lambda i: (i, 0)),
                  pl.BlockSpec((N_EXPERTS, d), lambda i: (0, 0)),
                  pl.BlockSpec((N_EXPERTS, 1), lambda i: (0, 0))],
        out_specs=[slot, slot, slot, pl.BlockSpec((N_EXPERTS, 1), lambda i: (0, 0))],
        out_shape=[jax.ShapeDtypeStruct((TOP_K, n), I32), jax.ShapeDtypeStruct((TOP_K, n), F32),
                   jax.ShapeDtypeStruct((TOP_K, n), I32), jax.ShapeDtypeStruct((N_EXPERTS, 1), F32)],
        compiler_params=_params(("arbitrary",)),
        name="moe_route",
    )(h, rwt, rb)


def _dest_kernel(e_ref, pos_ref, start_ref, d_ref):
    tm = e_ref.shape[1]
    row = lax.broadcasted_iota(I32, (N_EXPERTS, tm), 0)
    start = start_ref[...]
    rows = []
    for k in range(TOP_K):
        base = jnp.sum(jnp.where(row == e_ref[k:k + 1], start, 0.0), axis=0, keepdims=True)
        rows.append(base.astype(I32) + pos_ref[k:k + 1])
    d_ref[...] = jnp.concatenate(rows, axis=0)


def _dest(e_t, pos_t, start, tm):
    n = e_t.shape[1]
    tm = min(tm, n)
    slot = pl.BlockSpec((TOP_K, tm), lambda i: (0, i))
    return pl.pallas_call(
        _dest_kernel,
        grid=(n // tm,),
        in_specs=[slot, slot, pl.BlockSpec((N_EXPERTS, 1), lambda i: (0, 0))],
        out_specs=slot,
        out_shape=jax.ShapeDtypeStruct((TOP_K, n), I32),
        compiler_params=_params(("arbitrary",)),
        name="moe_dest",
    )(e_t, pos_t, start)


def _dispatch_kernel(zblk_ref, dest_ref, h_ref, xs_out, zbuf, sem):
    tm = h_ref.shape[0]

    @pl.when(pl.program_id(0) == 0)
    def _():
        zbuf[...] = jnp.zeros_like(zbuf)

        def block_copy(j):
            start = pl.multiple_of(zblk_ref[j] * MOE_ROWS, MOE_ROWS)
            return pltpu.make_async_copy(zbuf, xs_out.at[pl.ds(start, MOE_ROWS)], sem)

        def issue_zero(j, c):
            @pl.when(zblk_ref[j] >= 0)
            def _():
                block_copy(j).start()
            return c

        def drain_zero(j, c):
            @pl.when(zblk_ref[j] >= 0)
            def _():
                block_copy(j).wait()
            return c

        lax.fori_loop(0, zblk_ref.shape[0], issue_zero, 0)
        lax.fori_loop(0, zblk_ref.shape[0], drain_zero, 0)

    def row_copy(n, d):
        return pltpu.make_async_copy(h_ref.at[pl.ds(n, 1)], xs_out.at[pl.ds(d, 1)], sem)

    def issue(n, c):
        for k in range(TOP_K):
            row_copy(n, dest_ref[n * TOP_K + k]).start()
        return c

    lax.fori_loop(0, tm, issue, 0)
    for k in range(TOP_K):
        pltpu.make_async_copy(h_ref, xs_out.at[pl.ds(0, tm)], sem).wait()


def _dispatch(zero_blocks, dest_flat, h, n_rows, tm):
    n, d = h.shape
    tm = min(tm, n)
    grid_spec = pltpu.PrefetchScalarGridSpec(
        num_scalar_prefetch=1,
        grid=(n // tm,),
        in_specs=[pl.BlockSpec((tm * TOP_K,), lambda i, zt: (i,), memory_space=pltpu.SMEM),
                  pl.BlockSpec((tm, d), lambda i, zt: (i, 0))],
        out_specs=pl.BlockSpec(memory_space=pl.ANY),
        scratch_shapes=[pltpu.VMEM((MOE_ROWS, d), F32), pltpu.SemaphoreType.DMA(())],
    )
    return pl.pallas_call(
        _dispatch_kernel,
        grid_spec=grid_spec,
        out_shape=jax.ShapeDtypeStruct((n_rows, d), F32),
        compiler_params=_params(("arbitrary",)),
        name="moe_dispatch",
    )(zero_blocks, dest_flat, h)


def _expert_kernel(be_ref, used_ref, x_ref, wg_ref, wu_ref, wd_ref, o_ref):
    i = pl.program_id(0)

    @pl.when(i < used_ref[0])
    def _():
        x = x_ref[...].astype(BF16)
        g = jnp.dot(x, wg_ref[0], preferred_element_type=F32)
        u = jnp.dot(x, wu_ref[0], preferred_element_type=F32)
        o_ref[...] = jnp.dot((_silu(g) * u).astype(BF16), wd_ref[0], preferred_element_type=F32)

    @pl.when(i >= used_ref[0])
    def _():
        o_ref[...] = jnp.zeros_like(o_ref)


def _experts(block_e, n_used, xs, wg, wu, wd):
    n_rows, d = xs.shape
    ff = wg.shape[2]
    grid_spec = pltpu.PrefetchScalarGridSpec(
        num_scalar_prefetch=2,
        grid=(n_rows // MOE_ROWS,),
        in_specs=[pl.BlockSpec((MOE_ROWS, d), lambda i, be, nu: (jnp.minimum(i, nu[0] - 1), 0)),
                  pl.BlockSpec((1, d, ff), lambda i, be, nu: (be[i], 0, 0)),
                  pl.BlockSpec((1, d, ff), lambda i, be, nu: (be[i], 0, 0)),
                  pl.BlockSpec((1, ff, d), lambda i, be, nu: (be[i], 0, 0))],
        out_specs=pl.BlockSpec((MOE_ROWS, d), lambda i, be, nu: (i, 0)),
    )
    return pl.pallas_call(
        _expert_kernel,
        grid_spec=grid_spec,
        out_shape=jax.ShapeDtypeStruct((n_rows, d), F32),
        compiler_params=_params(("arbitrary",)),
        name="moe_experts",
    )(block_e, n_used, xs, wg, wu, wd)


def _combine_kernel(dest_ref, w_ref, x1_ref, h2_ref, gate2_ref, sg_ref, su_ref, sd_ref, ys_hbm,
                    o_ref, gbuf, sem):
    tm = x1_ref.shape[0]

    def row_copy(n, k, d):
        return pltpu.make_async_copy(ys_hbm.at[pl.ds(d, 1)], gbuf.at[k, pl.ds(n, 1)], sem)

    def issue(n, c):
        for k in range(TOP_K):
            row_copy(n, k, dest_ref[n * TOP_K + k]).start()
        return c

    lax.fori_loop(0, tm, issue, 0)
    h = h2_ref[...].astype(BF16)
    g = jnp.dot(h, sg_ref[...], preferred_element_type=F32)
    u = jnp.dot(h, su_ref[...], preferred_element_type=F32)
    y = jnp.dot((_silu(g) * u).astype(BF16), sd_ref[...], preferred_element_type=F32)
    for k in range(TOP_K):
        pltpu.make_async_copy(ys_hbm.at[pl.ds(0, tm)], gbuf.at[k], sem).wait()
    w = w_ref[...]
    for k in range(TOP_K):
        y = y + w[:, k:k + 1] * gbuf[k]
    o_ref[...] = x1_ref[...] + gate2_ref[0] * y


def _combine(dest_flat, w, x1, h2, gate2, tokens_per_gate, sg, su, sd, ys, tm):
    n, d = x1.shape
    tm = min(tm, n)
    row = pl.BlockSpec((tm, d), lambda i: (i, 0))
    const = lambda a: pl.BlockSpec(a.shape, lambda i: (0, 0))
    if gate2.shape[1] == 1:
        gate_spec = pl.BlockSpec((1, 1, d), lambda i: ((i * tm) // tokens_per_gate, 0, 0))
    else:
        gate_spec = pl.BlockSpec((1, tm, d), lambda i: (0, i, 0))
    return pl.pallas_call(
        _combine_kernel,
        grid=(n // tm,),
        in_specs=[pl.BlockSpec((tm * TOP_K,), lambda i: (i,), memory_space=pltpu.SMEM),
                  pl.BlockSpec((tm, TOP_K), lambda i: (i, 0)),
                  row, row, gate_spec, const(sg), const(su), const(sd),
                  pl.BlockSpec(memory_space=pl.ANY)],
        out_specs=row,
        out_shape=jax.ShapeDtypeStruct((n, d), F32),
        scratch_shapes=[pltpu.VMEM((TOP_K, tm, d), F32), pltpu.SemaphoreType.DMA(())],
        compiler_params=_params(("arbitrary",)),
        name="moe_combine",
    )(dest_flat, w, x1, h2, gate2, sg, su, sd, ys)


def _moe(h2, x1, gate2, tokens_per_gate, mw):
    n = h2.shape[0]
    e_t, w_t, pos_t, counts = _route(h2, mw["rwt"], mw["rb"], 256)
    counts = counts.reshape(-1).astype(I32)
    padded = (counts + MOE_ROWS - 1) // MOE_ROWS * MOE_ROWS
    pad_end = jnp.cumsum(padded)
    n_blocks = n * TOP_K // MOE_ROWS + N_EXPERTS
    start = (pad_end - padded).astype(F32).reshape(N_EXPERTS, 1)
    dest = _dest(e_t, pos_t, start, 512).T.reshape(-1)
    block_e = jnp.minimum(jnp.searchsorted(pad_end, jnp.arange(n_blocks, dtype=I32) * MOE_ROWS, side="right"),
                          N_EXPERTS - 1).astype(I32)
    n_used = (pad_end[-1:] // MOE_ROWS).astype(I32)
    blocks = jnp.arange(n_blocks, dtype=I32)
    zero_blocks = jnp.concatenate([jnp.where(padded > 0, pad_end // MOE_ROWS - 1, -1).astype(I32),
                                   jnp.where(blocks >= n_used[0], blocks, -1)])
    xs = _dispatch(zero_blocks, dest, h2, n_blocks * MOE_ROWS, 256)
    ys = _experts(block_e, n_used, xs, mw["wg"], mw["wu"], mw["wd"])
    return _combine(dest, w_t.T, x1, h2, gate2, tokens_per_gate, mw["sg"], mw["su"], mw["sd"], ys, 128)


def _pack_w_in(w_in):
    n_gate = N_HEADS * 3
    kv0 = ATTN_DIM
    g0 = kv0 + 6 * KV_DIM
    w_rows = jnp.concatenate([w_in[:, kv0:g0], w_in[:, g0 + n_gate:]], axis=1).astype(BF16)
    gate = jnp.pad(w_in[:, g0:g0 + n_gate], ((0, 0), (0, GATE_ROWS - n_gate)))
    v_sel = w_in[:, kv0 + 3 * KV_DIM:kv0 + 4 * KV_DIM]
    v_win = w_in[:, kv0 + 5 * KV_DIM:kv0 + 6 * KV_DIM]
    w_feat = jnp.concatenate([w_in[:, :ATTN_DIM], v_sel, v_win, gate], axis=1).T.astype(BF16)
    return w_rows, w_feat


def _compress_weights(cmp_pe, cmp_w1, cmp_w2, g_kcmp):
    eye = jnp.eye(N_KV_HEADS, dtype=F32)
    sub = CMP_LEN // CMP_STRIDE

    def w1_cat(w1):
        w = w1.reshape(sub, CMP_STRIDE, HEAD_DIM, CMP_HID)
        mats = [jnp.einsum("sdh,gk->sgdkh", w[j], eye).reshape(CMP_STRIDE * KV_DIM, N_KV_HEADS * CMP_HID)
                for j in range(sub)]
        return jnp.concatenate(mats, axis=1).astype(BF16)

    def w2_blk(w2):
        return jnp.einsum("hd,gk->ghkd", w2, eye).reshape(N_KV_HEADS * CMP_HID, KV_DIM).astype(BF16)

    def pe_rows(pe):
        p = pe.reshape(sub, CMP_STRIDE, 1, HEAD_DIM)
        p = jnp.broadcast_to(p, (sub, CMP_STRIDE, N_KV_HEADS, HEAD_DIM)).reshape(sub, 1, CMP_STRIDE * KV_DIM)
        return jnp.broadcast_to(p, (sub, 8, CMP_STRIDE * KV_DIM))

    assert sub == 2
    return {
        "w1k": w1_cat(cmp_w1[0]), "w1v": w1_cat(cmp_w1[1]),
        "w2k": w2_blk(cmp_w2[0]), "w2v": w2_blk(cmp_w2[1]),
        "pe": jnp.concatenate([pe_rows(cmp_pe[0]), pe_rows(cmp_pe[1])], axis=0),
        "gk": jnp.tile(g_kcmp, N_KV_HEADS).reshape(1, KV_DIM),
    }


def _layer(l, x_p, x_s, c_all, caches, page_table, w):
    ckc, cvc, cks, cvs, ckw, cvw, sconv = caches
    b, s, d = x_p.shape
    db, ds = x_s.shape[:2]
    assert ds == 1, "one new token per sample sequence"
    n_pool = ckc.shape[1]
    t_pos = page_table.shape[1] * PAGE_ROWS
    hist = CONV_WIDTH - 1

    ada = _ada(c_all, w["w_ada"][l], w["b_ada"][l])
    mods = [ada[:, k * d:(k + 1) * d] for k in range(6)]
    mod_p = [m[:b, None] for m in mods]
    mod_s = [m[b:][None] for m in mods]

    qkg = w["qk_norm_g"][l]
    gq = jnp.tile(qkg[0], N_HEADS).reshape(-1, 1)
    gks = jnp.tile(qkg[2], N_KV_HEADS).reshape(1, -1)
    gkw = jnp.tile(qkg[3], N_KV_HEADS).reshape(1, -1)
    g1 = w["norm1_g"][l].reshape(1, d)
    g2 = w["norm2_g"][l].reshape(1, d)
    w_rows, w_feat = _pack_w_in(w["w_in"][l])
    cw = _compress_weights(w["cmp_pe"][l], w["cmp_w1"][l], w["cmp_w2"][l], qkg[1])
    wao, wco, wo = (w[k][l].astype(BF16) for k in ("w_attn_out", "w_conv_out", "w_o"))
    mw = {"rwt": w["router_w"][l].T.astype(BF16), "rb": w["router_b"][l].reshape(-1, 1),
          "wg": w["exp_w_gate"][l].astype(BF16), "wu": w["exp_w_up"][l].astype(BF16),
          "wd": w["exp_w_down"][l].astype(BF16), "sg": w["shared_w_gate"][l].astype(BF16),
          "su": w["shared_w_up"][l].astype(BF16), "sd": w["shared_w_down"][l].astype(BF16)}
    pages = lambda a: a.reshape(-1, PAGE_ROWS, KV_DIM)
    feature_major = lambda a: jnp.transpose(a, (0, 2, 3, 1))

    (qt, kc, vc, ks, vs, kw, vw, ksb, kwb, vst, vwt, gnt, u, bg, ma, mc) = _in_proj(
        x_p, mod_p[0], mod_p[1], g1, w_rows, w_feat, gq, gks, gkw, 256)
    own_pages = jnp.arange(b * s // PAGE_ROWS, dtype=I32).reshape(b, s // PAGE_ROWS)
    kcmp, vcmp = _compress(own_pages, pages(kc), pages(vc), cw)
    o_att = _nsa_prompt(qt, gnt, kcmp, vcmp, ksb, vst, kwb, vwt, 256, 256)
    x1, h2 = _merge(x_p, o_att, u, None, bg, ma, mc, w["conv_w"][l], wao, wco, wo,
                    mod_p[2], mod_p[3], mod_p[4], g2, 256)
    y_p = _moe(h2.reshape(b * s, d), x1.reshape(b * s, d), mod_p[5], s, mw).reshape(b, s, d)
    keep = min(WINDOW, s)
    heads = lambda a: a.reshape(a.shape[0], a.shape[1], N_KV_HEADS, HEAD_DIM)
    p_state = (heads(kc), heads(vc), heads(ks), heads(vs), heads(kw[:, -keep:]), heads(vw[:, -keep:]),
               u[:, -hist:])

    (qt, kc, vc, ks, vs, kw, vw, _, _, _, _, gnt, u, bg, ma, mc) = _in_proj(
        x_s.reshape(1, db, d), mod_s[0], mod_s[1], g1, w_rows, w_feat, gq, gks, gkw, 128)
    per_seq = lambda a: a.reshape(db, 1, a.shape[-1])
    token_major = lambda a: jnp.swapaxes(a[0], 1, 2).reshape(db, a.shape[2]).astype(F32)
    q = token_major(qt)
    gn = jnp.pad(token_major(gnt), ((0, 0), (0, GATE_PAD - GATE_ROWS)))
    kcmp, vcmp = _compress(page_table, pages(ckc[l]), pages(cvc[l]), cw)
    o_cmp, idx = _sample_select(per_seq(q), kcmp, vcmp, t_pos)
    w_buf = ckw.shape[2]
    o_att = _sample_attend(idx, page_table, per_seq(q), per_seq(ks), per_seq(vs),
                           ckw[l].reshape(db, w_buf, KV_DIM), cvw[l].reshape(db, w_buf, KV_DIM),
                           per_seq(kw), per_seq(vw), per_seq(gn), o_cmp, feature_major(cks[l]),
                           feature_major(cvs[l]), t_pos)
    x1, h2 = _merge(x_s.reshape(1, db, d), o_att.reshape(1, db, ATTN_DIM), u, jnp.swapaxes(sconv[l], 0, 1),
                    bg, ma, mc, w["conv_w"][l], wao, wco, wo, mod_s[2], mod_s[3], mod_s[4], g2, 128)
    y_s = _moe(h2.reshape(db, d), x1.reshape(db, d), mod_s[5], 1, mw).reshape(db, 1, d)
    new_row = lambda a: a.reshape(db, 1, N_KV_HEADS, HEAD_DIM)
    s_state = (new_row(kc), new_row(vc), new_row(ks), new_row(vs),
               jnp.concatenate([ckw[l], new_row(kw)], axis=1)[:, -w_buf:],
               jnp.concatenate([cvw[l], new_row(vw)], axis=1)[:, -w_buf:],
               jnp.concatenate([sconv[l], u.reshape(db, 1, CONV_DIM)], axis=1)[:, -hist:])
    return y_p, y_s, p_state + s_state


def kernel(x_prompt, x_sample, cache_k_cmp, cache_v_cmp, cache_k_sel, cache_v_sel, cache_k_win, cache_v_win,
           state_conv, page_table, c_prompt, c_sample, w_ada, b_ada, norm1_g, norm2_g, w_in, qk_norm_g,
           cmp_pe, cmp_w1, cmp_w2, conv_w, w_attn_out, w_conv_out, w_o, router_w, router_b,
           exp_w_gate, exp_w_up, exp_w_down, shared_w_gate, shared_w_up, shared_w_down):
    w = dict(w_ada=w_ada, b_ada=b_ada, norm1_g=norm1_g, norm2_g=norm2_g, w_in=w_in, qk_norm_g=qk_norm_g,
             cmp_pe=cmp_pe, cmp_w1=cmp_w1, cmp_w2=cmp_w2, conv_w=conv_w, w_attn_out=w_attn_out,
             w_conv_out=w_conv_out, w_o=w_o, router_w=router_w, router_b=router_b, exp_w_gate=exp_w_gate,
             exp_w_up=exp_w_up, exp_w_down=exp_w_down, shared_w_gate=shared_w_gate, shared_w_up=shared_w_up,
             shared_w_down=shared_w_down)
    caches = (cache_k_cmp, cache_v_cmp, cache_k_sel, cache_v_sel, cache_k_win, cache_v_win, state_conv)
    c_all = jnp.concatenate([c_prompt, c_sample], axis=0)
    x_p, x_s = x_prompt, x_sample
    states = []
    for l in range(w_ada.shape[0]):
        x_p, x_s, st = _layer(l, x_p, x_s, c_all, caches, page_table.astype(I32), w)
        states.append(st)
    return (x_p, x_s) + tuple(jnp.stack(s) for s in zip(*states))
```

```python
import functools

import jax
import jax.numpy as jnp
from jax import lax
from jax.experimental import pallas as pl
from jax.experimental.pallas import tpu as pltpu

F32 = jnp.float32
BF16 = jnp.bfloat16
I32 = jnp.int32

D_MODEL = 1024
N_HEADS = 8
HEAD_DIM = 64
N_KV_HEADS = 2
GROUP = N_HEADS // N_KV_HEADS
ATTN_DIM = N_HEADS * HEAD_DIM
KV_DIM = N_KV_HEADS * HEAD_DIM
ATTN_SCALE = HEAD_DIM ** -0.5
CMP_LEN = 32
CMP_STRIDE = 16
CMP_HID = 4 * HEAD_DIM
SEL_BLOCK = 64
SEL_TOP = 16
FORCED_SCORE = 1e6
WINDOW = 512
CONV_DIM = D_MODEL // 2
CONV_WIDTH = 3
N_EXPERTS = 256
TOP_K = 8
N_GROUPS = 8
TOPK_GROUPS = 4
EXPERT_FF = D_MODEL // 4
ROUTED_SCALE = 2.5
NORM_EPS = 1e-6
PAGE_ROWS = 128
CHUNKS_PER_PAGE = PAGE_ROWS // CMP_STRIDE
NEG = -1e30
GATE_PAD = 128
MOE_ROWS = 256
VMEM_LIMIT = 56 * 1024 * 1024


def _params(sem):
    return pltpu.CompilerParams(dimension_semantics=sem, vmem_limit_bytes=VMEM_LIMIT)


def _bdot(a, b):
    return jnp.dot(a.astype(BF16), b.astype(BF16), preferred_element_type=F32)


def _bdot_nt(a, b):
    return lax.dot_general(a.astype(BF16), b.astype(BF16), (((1,), (1,)), ((), ())),
                           preferred_element_type=F32)


def _split_dot(a, b_bf16, nt=False):
    hi = a.astype(BF16)
    lo = (a - hi.astype(F32)).astype(BF16)
    f = _bdot_nt if nt else _bdot
    return f(hi, b_bf16) + f(lo, b_bf16)


def _sigmoid(x):
    return 1.0 / (1.0 + jnp.exp(-x))


def _silu(x):
    return x * _sigmoid(x)


def _gelu_tanh(x):
    return 0.5 * x * (1.0 + jnp.tanh(0.7978845608028654 * (x + 0.044715 * (x * x * x))))


def _head_rms(z, gain):
    r = lax.broadcasted_iota(I32, (128, 128), 0) // HEAD_DIM
    c = lax.broadcasted_iota(I32, (128, 128), 1) // HEAD_DIM
    seg = (r == c).astype(BF16)
    x2 = z * z
    parts = [_split_dot(x2[:, j:j + 128], seg) for j in range(0, z.shape[1], 128)]
    ss = parts[0] if len(parts) == 1 else jnp.concatenate(parts, axis=1)
    return z * lax.rsqrt(ss * (1.0 / HEAD_DIM) + NORM_EPS) * gain


def _rms_mod(x, g, shift, scale):
    y = x * lax.rsqrt(jnp.mean(x * x, axis=-1, keepdims=True) + NORM_EPS) * g
    return y * (1.0 + scale) + shift


def _ada_kernel(c_ref, w_ref, b_ref, o_ref):
    o_ref[...] = _bdot(_silu(c_ref[...]), w_ref[...]) + b_ref[...]


def _ada(c, w, b):
    m, d = c.shape
    n = w.shape[1]
    tn = 1024
    return pl.pallas_call(
        _ada_kernel,
        grid=(n // tn,),
        in_specs=[pl.BlockSpec((m, d), lambda j: (0, 0)),
                  pl.BlockSpec((d, tn), lambda j: (0, j)),
                  pl.BlockSpec((1, tn), lambda j: (0, j))],
        out_specs=pl.BlockSpec((m, tn), lambda j: (0, j)),
        out_shape=jax.ShapeDtypeStruct((m, n), F32),
        compiler_params=_params(("arbitrary",)),
        name="ada_ln",
    )(c, w, b.reshape(1, n))


_C_KV = 0
_C_X = 6 * KV_DIM
_C_B = _C_X + CONV_DIM
_C_C = _C_B + CONV_DIM
_C_MA = _C_C + CONV_DIM
_C_MC = _C_MA + D_MODEL
_R_Q = 0
_R_VS = ATTN_DIM
_R_VW = _R_VS + KV_DIM
_R_G = _R_VW + KV_DIM
GATE_ROWS = 32
_R_END = _R_G + GATE_ROWS
TOK_TILE = 128


def _inproj_kernel(x_ref, shift_ref, scale_ref, g1_ref, w_ref, wt_ref, gq_ref, gks_ref, gkw_ref,
                   qt_ref, kc_ref, vc_ref, ks_ref, vs_ref, kw_ref, vw_ref, ksb_ref, kwb_ref,
                   vst_ref, vwt_ref, gnt_ref, u_ref, bg_ref, ma_ref, mc_ref):
    h = _rms_mod(x_ref[0], g1_ref[...], shift_ref[0], scale_ref[0]).astype(BF16)

    def proj(c0, n):
        return jnp.dot(h, w_ref[:, c0:c0 + n], preferred_element_type=F32)

    kc_ref[0] = proj(_C_KV, KV_DIM)
    vc_ref[0] = proj(_C_KV + KV_DIM, KV_DIM)
    ks = _head_rms(proj(_C_KV + 2 * KV_DIM, KV_DIM), gks_ref[...])
    ks_ref[0] = ks
    ksb_ref[0] = ks.astype(BF16)
    vs_ref[0] = proj(_C_KV + 3 * KV_DIM, KV_DIM)
    kw = _head_rms(proj(_C_KV + 4 * KV_DIM, KV_DIM), gkw_ref[...])
    kw_ref[0] = kw
    kwb_ref[0] = kw.astype(BF16)
    vw_ref[0] = proj(_C_KV + 5 * KV_DIM, KV_DIM)
    u_ref[0] = proj(_C_C, CONV_DIM) * proj(_C_X, CONV_DIM)
    bg_ref[0] = proj(_C_B, CONV_DIM)
    ma_ref[0] = _sigmoid(proj(_C_MA, D_MODEL))
    mc_ref[0] = _sigmoid(proj(_C_MC, D_MODEL))

    gq = gq_ref[...]
    for r in range(h.shape[0] // TOK_TILE):
        zt = _bdot_nt(wt_ref[...], h[r * TOK_TILE:(r + 1) * TOK_TILE])
        heads = []
        for hd in range(N_HEADS):
            z = zt[HEAD_DIM * hd:HEAD_DIM * (hd + 1)]
            ms = jnp.sum(z * z, axis=0, keepdims=True) * (1.0 / HEAD_DIM)
            heads.append(z * lax.rsqrt(ms + NORM_EPS) * gq[HEAD_DIM * hd:HEAD_DIM * (hd + 1)] * ATTN_SCALE)
        qt_ref[0, r] = jnp.concatenate(heads, axis=0).astype(BF16)
        vst_ref[0, r] = zt[_R_VS:_R_VS + KV_DIM].astype(BF16)
        vwt_ref[0, r] = zt[_R_VW:_R_VW + KV_DIM].astype(BF16)
        gnt_ref[0, r] = _sigmoid(zt[_R_G:_R_G + GATE_ROWS])


def _in_proj(x, shift, scale, g1, w_rows, w_feat, gq, gks, gkw, tm):
    b, s, d = x.shape
    tm = min(tm, s)
    assert s % tm == 0 and tm % TOK_TILE == 0
    r = shift.shape[1]
    rb = 1 if r == 1 else tm
    mod_map = (lambda i, j: (i, 0, 0)) if r == 1 else (lambda i, j: (i, j, 0))
    row = lambda n: pl.BlockSpec((1, tm, n), lambda i, j: (i, j, 0))
    feat = lambda n: pl.BlockSpec((1, tm // TOK_TILE, n, TOK_TILE), lambda i, j: (i, j, 0, 0))
    const = lambda a: pl.BlockSpec(a.shape, lambda i, j: (0, 0))
    rows_f32 = lambda n: jax.ShapeDtypeStruct((b, s, n), F32)
    rows_b16 = lambda n: jax.ShapeDtypeStruct((b, s, n), BF16)
    feat_sh = lambda n, dt: jax.ShapeDtypeStruct((b, s // TOK_TILE, n, TOK_TILE), dt)
    out_specs = ([feat(ATTN_DIM)] + [row(KV_DIM)] * 8 + [feat(KV_DIM), feat(KV_DIM), feat(GATE_ROWS)]
                 + [row(CONV_DIM), row(CONV_DIM), row(d), row(d)])
    out_shape = ([feat_sh(ATTN_DIM, BF16)] + [rows_f32(KV_DIM)] * 6 + [rows_b16(KV_DIM)] * 2
                 + [feat_sh(KV_DIM, BF16), feat_sh(KV_DIM, BF16), feat_sh(GATE_ROWS, F32)]
                 + [rows_f32(CONV_DIM), rows_f32(CONV_DIM), rows_f32(d), rows_f32(d)])
    return pl.pallas_call(
        _inproj_kernel,
        grid=(b, s // tm),
        in_specs=[row(d), pl.BlockSpec((1, rb, d), mod_map), pl.BlockSpec((1, rb, d), mod_map),
                  const(g1), const(w_rows), const(w_feat), const(gq), const(gks), const(gkw)],
        out_specs=out_specs,
        out_shape=out_shape,
        compiler_params=_params(("arbitrary", "arbitrary")),
        name="in_proj",
    )(x, shift, scale, g1, w_rows, w_feat, gq, gks, gkw)


def _compress_kernel(pt_ref, kpool, vpool, w1k_ref, w1v_ref, pe_ref, w2k_ref, w2v_ref, gk_ref,
                     kcmp_ref, vcmp_ref, kbuf, vbuf, sem, *, n_pages):
    b = pl.program_id(0)
    nb = pl.num_programs(0)
    slot = b % 2

    def page_copy(pool, buf, seq, p, sl, which):
        return pltpu.make_async_copy(pool.at[pt_ref[seq, p]], buf.at[sl, p], sem.at[which, sl])

    def fetch(seq, sl):
        for p in range(n_pages):
            page_copy(kpool, kbuf, seq, p, sl, 0).start()
            page_copy(vpool, vbuf, seq, p, sl, 1).start()

    @pl.when(b == 0)
    def _():
        fetch(0, 0)

    @pl.when(b + 1 < nb)
    def _():
        fetch(b + 1, 1 - slot)

    for p in range(n_pages):
        page_copy(kpool, kbuf, b, p, slot, 0).wait()
        page_copy(vpool, vbuf, b, p, slot, 1).wait()

    n_chunk = n_pages * CHUNKS_PER_PAGE
    half = N_KV_HEADS * CMP_HID

    def summarise(buf, w1_ref, w2_ref, pe0, pe1):
        a = jnp.zeros((n_chunk, 2 * half), F32)
        for s in range(0, CMP_STRIDE, 2):
            x = jnp.concatenate(
                [buf[slot, :, pl.ds(s + j, CHUNKS_PER_PAGE, stride=CMP_STRIDE), :].reshape(n_chunk, KV_DIM)
                 for j in range(2)], axis=1).astype(BF16)
            a = a + jnp.dot(x, w1_ref[s * KV_DIM:(s + 2) * KV_DIM, :], preferred_element_type=F32)
        pe_term = (jnp.dot(pe0.astype(BF16), w1_ref[:, :half], preferred_element_type=F32)
                   + jnp.dot(pe1.astype(BF16), w1_ref[:, half:], preferred_element_type=F32))
        nxt = pltpu.roll(a[:, half:], n_chunk - 1, 0)
        hid = a[:, :half] + nxt + pe_term[0:1]
        return jnp.dot(_gelu_tanh(hid).astype(BF16), w2_ref[...], preferred_element_type=F32)

    kcmp_ref[0] = _head_rms(summarise(kbuf, w1k_ref, w2k_ref, pe_ref[0], pe_ref[1]), gk_ref[...])
    vcmp_ref[0] = summarise(vbuf, w1v_ref, w2v_ref, pe_ref[2], pe_ref[3])


def _compress(page_table, kpool, vpool, cw):
    n_seq, n_pages = page_table.shape
    n_chunk = n_pages * CHUNKS_PER_PAGE
    kp, vp = kpool, vpool
    const = lambda a: pl.BlockSpec(a.shape, lambda i, pt: (0,) * a.ndim)
    out = pl.BlockSpec((1, n_chunk, KV_DIM), lambda i, pt: (i, 0, 0))
    grid_spec = pltpu.PrefetchScalarGridSpec(
        num_scalar_prefetch=1,
        grid=(n_seq,),
        in_specs=[pl.BlockSpec(memory_space=pl.ANY), pl.BlockSpec(memory_space=pl.ANY),
                  const(cw["w1k"]), const(cw["w1v"]), const(cw["pe"]),
                  const(cw["w2k"]), const(cw["w2v"]), const(cw["gk"])],
        out_specs=[out, out],
        scratch_shapes=[pltpu.VMEM((2, n_pages, PAGE_ROWS, KV_DIM), F32),
                        pltpu.VMEM((2, n_pages, PAGE_ROWS, KV_DIM), F32),
                        pltpu.SemaphoreType.DMA((2, 2))],
    )
    return pl.pallas_call(
        functools.partial(_compress_kernel, n_pages=n_pages),
        grid_spec=grid_spec,
        out_shape=[jax.ShapeDtypeStruct((n_seq, n_chunk, KV_DIM), F32)] * 2,
        compiler_params=_params(("arbitrary",)),
        name="compress",
    )(page_table, kp, vp, cw["w1k"], cw["w1v"], cw["pe"], cw["w2k"], cw["w2v"], cw["gk"])


def _overlap(n_cmp, n_slc):
    c = lax.broadcasted_iota(I32, (n_cmp, n_slc), 0) * CMP_STRIDE
    j = lax.broadcasted_iota(I32, (n_cmp, n_slc), 1) * SEL_BLOCK
    return ((c < j + SEL_BLOCK) & (c + CMP_LEN > j)).astype(BF16)


def _overlap_t(n_slc, n_cmp):
    j = lax.broadcasted_iota(I32, (n_slc, n_cmp), 0) * SEL_BLOCK
    c = lax.broadcasted_iota(I32, (n_slc, n_cmp), 1) * CMP_STRIDE
    return ((c < j + SEL_BLOCK) & (c + CMP_LEN > j)).astype(BF16)


def _nsa_prompt_kernel(qt_ref, gnt_ref, kcmp_ref, vcmp_ref, ksb_ref, vst_ref, kwb_ref, vwt_ref, o_ref,
                       qp_ref, selb_ref, rowb_ref, acc_ref, ot_ref, *, tq, tk, n_slc):
    i = pl.program_id(1)
    t0 = i * tq
    n_cmp = kcmp_ref.shape[1]
    n_top = min(SEL_TOP, n_slc)
    width = N_HEADS * tq
    gw = GROUP * tq
    slopes = [2.0 ** -(h + 1) for h in range(N_HEADS)]
    slope_row = jnp.concatenate([jnp.full((1, tq), s, F32) for s in slopes], axis=1)
    lane_t = t0 + lax.broadcasted_iota(I32, (1, tq), 1)
    gnt = jnp.concatenate([gnt_ref[0, r] for r in range(tq // TOK_TILE)], axis=1)
    gate = lambda h, j: gnt[3 * h + j:3 * h + j + 1]
    head = lambda a, h: a[:, h * tq:(h + 1) * tq]

    zero = jnp.zeros((HEAD_DIM, TOK_TILE), BF16)
    for h in range(N_HEADS):
        for r in range(tq // TOK_TILE):
            qh = qt_ref[0, r, HEAD_DIM * h:HEAD_DIM * (h + 1), :]
            c0 = h * tq + r * TOK_TILE
            qp_ref[:, c0:c0 + TOK_TILE] = jnp.concatenate([qh, zero] if h < GROUP else [zero, qh], axis=0)
    rowb_ref[...] = lax.broadcasted_iota(I32, (tk, width), 0).astype(F32) * slope_row

    kcb = kcmp_ref[0].astype(BF16)
    vct = vcmp_ref[0].T.astype(BF16)
    c_col = lax.broadcasted_iota(I32, (n_cmp, 1), 0) * CMP_STRIDE
    dist = lane_t.astype(F32) - (c_col.astype(F32) + (CMP_LEN - 1) / 2)
    vis = c_col + (CMP_LEN - 1) <= lane_t
    j_col = lax.broadcasted_iota(I32, (n_slc, 1), 0)
    cur = lane_t // SEL_BLOCK
    forced = (j_col == 0) | (j_col == cur) | (j_col == cur - 1)
    valid = j_col * SEL_BLOCK <= lane_t
    sub = lax.broadcasted_iota(I32, (8, 1), 0)
    sc_all = jnp.dot(kcb, qp_ref[...], preferred_element_type=F32)
    probs = []
    for h in range(N_HEADS):
        sc = jnp.where(vis, head(sc_all, h) - slopes[h] * dist, NEG)
        m = jnp.max(sc, axis=0, keepdims=True)
        e = jnp.where(vis, jnp.exp(sc - m), 0.0)
        probs.append(e / jnp.maximum(jnp.sum(e, axis=0, keepdims=True), 1e-30))
    o_cmp = jnp.dot(vct, jnp.concatenate(probs, axis=1).astype(BF16), preferred_element_type=F32)
    for h in range(N_HEADS):
        g = h // GROUP
        ot_ref[HEAD_DIM * h:HEAD_DIM * (h + 1)] = gate(h, 0) * head(o_cmp[HEAD_DIM * g:HEAD_DIM * (g + 1)], h)
    for g in range(N_KV_HEADS):
        psum = probs[GROUP * g]
        for r in range(1, GROUP):
            psum = psum + probs[GROUP * g + r]
        hi = psum.astype(BF16)
        lo = (psum - hi.astype(F32)).astype(BF16)
        ovt = _overlap_t(n_slc, n_cmp)
        imp = (jnp.dot(ovt, hi, preferred_element_type=F32)
               + jnp.dot(ovt, lo, preferred_element_type=F32))
        imp = jnp.where(valid, jnp.where(forced, FORCED_SCORE, imp), -1.0)
        groups = [imp[8 * v:8 * (v + 1)] for v in range(n_slc // 8)]
        rank = [jnp.zeros((8, tq), F32) for _ in groups]
        for b in range(n_slc):
            row = imp[b:b + 1]
            for v, gv in enumerate(groups):
                if v < b // 8:
                    ahead = row > gv
                elif v > b // 8:
                    ahead = row >= gv
                else:
                    ahead = (row > gv) | ((row == gv) & (sub > b % 8))
                rank[v] = rank[v] + jnp.where(ahead, 1.0, 0.0)
        rank = jnp.concatenate(rank, axis=0)
        selb_ref[g] = jnp.where((rank < n_top) & valid, 0.0, NEG)

    half = SEL_BLOCK

    def tile_step(kt, carry, k_ref, vt_ref, use_sel, causal, window):
        m_i, l_i = carry
        s0 = pl.multiple_of(kt * tk, tk)
        k = k_ref[0, pl.ds(s0, tk), :]
        n_sub = tk // TOK_TILE
        vt = jnp.concatenate([vt_ref[0, kt * n_sub + j] for j in range(n_sub)], axis=1)
        off = t0 - s0
        s = jnp.dot(k, qp_ref[...], preferred_element_type=F32) + rowb_ref[...]
        base = slope_row * (-off.astype(F32))
        if use_sel:
            blk = kt * (tk // SEL_BLOCK)
            rows = []
            for j in range(tk // SEL_BLOCK):
                sel = [selb_ref[g, pl.ds(blk + j, 1), :] for g in range(N_KV_HEADS)]
                rows.append(base + jnp.concatenate([sel[h // GROUP] for h in range(N_HEADS)], axis=1))
            s = jnp.concatenate([s[j * half:(j + 1) * half] + rows[j] for j in range(tk // SEL_BLOCK)], axis=0)
        else:
            s = s + base
        if causal or window:
            d_kq = (lax.broadcasted_iota(I32, (tk, width), 0)
                    - (lax.broadcasted_iota(I32, (tk, width), 1) & (tq - 1)))
            bad = None
            if causal:
                bad = d_kq > off
            if window:
                late = d_kq < off - WINDOW
                bad = late if bad is None else (bad | late)
            s = jnp.where(bad, NEG, s)
        m_new = jnp.maximum(m_i, jnp.max(s, axis=0, keepdims=True))
        alpha = jnp.exp(m_i - m_new)
        p = jnp.exp(s - m_new)
        l_new = alpha * l_i + jnp.sum(p, axis=0, keepdims=True)
        pb = p.astype(BF16)
        pv = jnp.concatenate(
            [jnp.dot(vt[HEAD_DIM * g:HEAD_DIM * (g + 1)], pb[:, g * gw:(g + 1) * gw], preferred_element_type=F32)
             for g in range(N_KV_HEADS)], axis=1)
        acc_ref[...] = alpha * acc_ref[...] + pv
        return m_new, l_new

    def branch(k_ref, vt_ref, use_sel, lo, mid, hi, gate_idx):
        acc_ref[...] = jnp.zeros_like(acc_ref)
        carry = (jnp.full((1, width), NEG, F32), jnp.zeros((1, width), F32))
        window = not use_sel
        carry = lax.fori_loop(lo, mid, lambda kt, c: tile_step(kt, c, k_ref, vt_ref, use_sel, False, window),
                              carry)
        carry = lax.fori_loop(mid, hi, lambda kt, c: tile_step(kt, c, k_ref, vt_ref, use_sel, True, window),
                              carry)
        out = acc_ref[...] / jnp.maximum(carry[1], 1e-30)
        for h in range(N_HEADS):
            rows = slice(HEAD_DIM * h, HEAD_DIM * (h + 1))
            ot_ref[rows] = ot_ref[rows] + gate(h, gate_idx) * head(out, h)

    first_now = t0 // tk
    end = (t0 + tq) // tk
    branch(ksb_ref, vst_ref, True, 0, first_now, end, 1)
    branch(kwb_ref, vwt_ref, False, jnp.maximum(first_now - WINDOW // tk, 0), first_now, end, 2)
    o_ref[0] = ot_ref[...].T


def _nsa_prompt(qt, gnt, kcmp, vcmp, ksb, vst, kwb, vwt, tq, tk):
    b, n_tile, _, _ = qt.shape
    s = n_tile * TOK_TILE
    tq = min(tq, s)
    tk = min(tk, tq)
    assert s % tq == 0 and tq % tk == 0 and tk % TOK_TILE == 0 and WINDOW % tk == 0 and tq & (tq - 1) == 0
    n_slc = -(-s // SEL_BLOCK)
    assert n_slc % 8 == 0
    n_cmp = kcmp.shape[1]
    nq = tq // TOK_TILE
    tile = lambda n: pl.BlockSpec((1, nq, n, TOK_TILE), lambda i, j: (i, j, 0, 0))
    seq = lambda n, w: pl.BlockSpec((1, n, w), lambda i, j: (i, 0, 0))
    feat = pl.BlockSpec((1, n_tile, KV_DIM, TOK_TILE), lambda i, j: (i, 0, 0, 0))
    return pl.pallas_call(
        functools.partial(_nsa_prompt_kernel, tq=tq, tk=tk, n_slc=n_slc),
        grid=(b, s // tq),
        in_specs=[tile(ATTN_DIM), tile(GATE_ROWS), seq(n_cmp, KV_DIM), seq(n_cmp, KV_DIM),
                  seq(s, KV_DIM), feat, seq(s, KV_DIM), feat],
        out_specs=pl.BlockSpec((1, tq, ATTN_DIM), lambda i, j: (i, j, 0)),
        out_shape=jax.ShapeDtypeStruct((b, s, ATTN_DIM), F32),
        scratch_shapes=[pltpu.VMEM((KV_DIM, N_HEADS * tq), BF16),
                        pltpu.VMEM((N_KV_HEADS, n_slc, tq), F32),
                        pltpu.VMEM((tk, N_HEADS * tq), F32),
                        pltpu.VMEM((HEAD_DIM, N_HEADS * tq), F32),
                        pltpu.VMEM((ATTN_DIM, tq), F32)],
        compiler_params=_params(("arbitrary", "arbitrary")),
        name="nsa_prompt",
    )(qt, gnt, kcmp, vcmp, ksb, vst, kwb, vwt)


def _query_rows(q):
    rows = []
    for h in range(N_HEADS):
        qh = q[:, HEAD_DIM * h:HEAD_DIM * (h + 1)]
        z = jnp.zeros_like(qh)
        rows.append(jnp.concatenate([qh, z] if h < GROUP else [z, qh], axis=1))
    return jnp.concatenate(rows, axis=0)


def _head_slopes():
    return jnp.concatenate([jnp.full((1, 1), 2.0 ** -(h + 1), F32) for h in range(N_HEADS)], axis=0)


def _sample_select_kernel(q_ref, kcmp_ref, vcmp_ref, ocmp_ref, idx_ref, *, t_pos, n_slc, n_lane):
    qr = _query_rows(q_ref[0])
    n_cmp = kcmp_ref.shape[1]
    slope = _head_slopes()
    sc = _bdot_nt(qr, kcmp_ref[0])
    c_start = lax.broadcasted_iota(I32, (1, n_cmp), 1) * CMP_STRIDE
    dist = float(t_pos) - (c_start.astype(F32) + (CMP_LEN - 1) / 2)
    vis = c_start + (CMP_LEN - 1) <= t_pos
    sc = jnp.where(vis, sc - slope * dist, NEG)
    m = jnp.max(sc, axis=-1, keepdims=True)
    e = jnp.where(vis, jnp.exp(sc - m), 0.0)
    p = e / jnp.maximum(jnp.sum(e, axis=-1, keepdims=True), 1e-30)
    ocmp_ref[0] = _bdot(p, vcmp_ref[0])

    hrow = lax.broadcasted_iota(I32, (N_HEADS, 1), 0)
    psum = jnp.concatenate(
        [jnp.sum(jnp.where((hrow // GROUP) == g, p, 0.0), axis=0, keepdims=True) for g in range(N_KV_HEADS)]
        + [jnp.zeros((8 - N_KV_HEADS, n_cmp), F32)], axis=0)
    imp = _split_dot(psum, _overlap(n_cmp, n_lane))
    j_row = lax.broadcasted_iota(I32, (1, n_lane), 1)
    cur = t_pos // SEL_BLOCK
    forced = (j_row == 0) | (j_row == cur) | (j_row == cur - 1)
    valid = (j_row * SEL_BLOCK <= t_pos) & (j_row < n_slc)
    imp = jnp.where(valid, jnp.where(forced, FORCED_SCORE, imp), -1.0)
    n_top = min(SEL_TOP, n_slc)
    ii = lax.broadcasted_iota(I32, (n_lane, n_lane), 0)
    jj = lax.broadcasted_iota(I32, (n_lane, n_lane), 1)
    slot = lax.broadcasted_iota(I32, (1, SEL_TOP), 1)
    rows = []
    for g in range(N_KV_HEADS):
        by_lane = jnp.broadcast_to(imp[g:g + 1], (n_lane, n_lane))
        col = jnp.sum(jnp.where(ii == jj, by_lane, 0.0), axis=1, keepdims=True)
        by_row = jnp.broadcast_to(col, (n_lane, n_lane))
        beaten = (by_lane > by_row) | ((by_lane == by_row) & (jj < ii))
        rank_col = jnp.sum(beaten.astype(F32), axis=1, keepdims=True)
        hit = rank_col == slot.astype(F32)
        block = jnp.sum(jnp.where(hit, ii[:, :SEL_TOP].astype(F32), 0.0), axis=0, keepdims=True)
        rows.append(jnp.where(slot < n_top, block, float(n_lane - 1)).astype(I32))
    idx_ref[0] = jnp.concatenate(rows, axis=0)


def _sample_select(q, kcmp, vcmp, t_pos):
    db = q.shape[0]
    n_cmp = kcmp.shape[1]
    n_slc = -(-(t_pos + 1) // SEL_BLOCK)
    n_lane = -(-n_slc // 128) * 128
    return pl.pallas_call(
        functools.partial(_sample_select_kernel, t_pos=t_pos, n_slc=n_slc, n_lane=n_lane),
        grid=(db,),
        in_specs=[pl.BlockSpec((1, 1, ATTN_DIM), lambda i: (i, 0, 0)),
                  pl.BlockSpec((1, n_cmp, KV_DIM), lambda i: (i, 0, 0)),
                  pl.BlockSpec((1, n_cmp, KV_DIM), lambda i: (i, 0, 0))],
        out_specs=[pl.BlockSpec((1, N_HEADS, KV_DIM), lambda i: (i, 0, 0)),
                   pl.BlockSpec((1, N_KV_HEADS, SEL_TOP), lambda i: (i, 0, 0))],
        out_shape=[jax.ShapeDtypeStruct((db, N_HEADS, KV_DIM), F32),
                   jax.ShapeDtypeStruct((db, N_KV_HEADS, SEL_TOP), I32)],
        compiler_params=_params(("arbitrary",)),
        name="sample_select",
    )(q, kcmp, vcmp)


def _sample_attend_kernel(idx_s, pt_s, q_ref, idx_ref, ksn_ref, vsn_ref, kwb_ref, vwb_ref, kwn_ref, vwn_ref,
                          gn_ref, ocmp_ref, kpool, vpool, o_ref, kbuf, vbuf, sem, *, t_pos, n_past_blk):
    b = pl.program_id(0)
    nb = pl.num_programs(0)
    slot = b % 2
    n_sel = N_KV_HEADS * SEL_TOP
    blk_per_page = PAGE_ROWS // SEL_BLOCK

    def page_copy(pool, buf, seq, s, sl, which):
        blk = jnp.minimum(idx_s[seq * n_sel + s], n_past_blk - 1)
        page = pt_s[seq, blk // blk_per_page]
        return pltpu.make_async_copy(pool.at[page, s // SEL_TOP], buf.at[sl, s], sem.at[which, sl])

    def fetch(seq, sl):
        for s in range(n_sel):
            page_copy(kpool, kbuf, seq, s, sl, 0).start()
            page_copy(vpool, vbuf, seq, s, sl, 1).start()

    @pl.when(b == 0)
    def _():
        fetch(0, 0)

    @pl.when(b + 1 < nb)
    def _():
        fetch(b + 1, 1 - slot)

    for s in range(n_sel):
        page_copy(kpool, kbuf, b, s, slot, 0).wait()
        page_copy(vpool, vbuf, b, s, slot, 1).wait()

    q = q_ref[0]
    qr = _query_rows(q)
    slope = _head_slopes()
    hrow = lax.broadcasted_iota(I32, (N_HEADS, 1), 0)
    own_half = lambda x: jnp.where(hrow < GROUP, x[:, :HEAD_DIM], x[:, HEAD_DIM:])

    n_col = SEL_TOP * PAGE_ROWS
    idx8 = jnp.concatenate([idx_ref[0].astype(F32), jnp.zeros((8 - N_KV_HEADS, SEL_TOP), F32)], axis=0)
    expand = (lax.broadcasted_iota(I32, (SEL_TOP, n_col), 0)
              == lax.broadcasted_iota(I32, (SEL_TOP, n_col), 1) // PAGE_ROWS)
    blk8 = _bdot(idx8, expand.astype(BF16)).astype(I32)
    col = lax.broadcasted_iota(I32, (1, n_col), 1)
    row_in_page = col % PAGE_ROWS
    cur = float(t_pos // SEL_BLOCK)
    zero_rows = jnp.zeros((8 - GROUP, HEAD_DIM), F32)
    o_sel = []
    for g in range(N_KV_HEADS):
        kt = jnp.concatenate([kbuf[slot, g * SEL_TOP + s] for s in range(SEL_TOP)], axis=1)
        vt = jnp.concatenate([vbuf[slot, g * SEL_TOP + s] for s in range(SEL_TOP)], axis=1)
        qg = jnp.concatenate([q[:, HEAD_DIM * h:HEAD_DIM * (h + 1)] for h in range(GROUP * g, GROUP * (g + 1))]
                             + [zero_rows], axis=0)
        sl_g = jnp.concatenate([jnp.full((1, 1), 2.0 ** -(GROUP * g + r + 1), F32) for r in range(GROUP)]
                               + [jnp.ones((8 - GROUP, 1), F32)], axis=0)
        blk = blk8[g:g + 1]
        mask = ((row_in_page // SEL_BLOCK) == (blk % blk_per_page)) & (blk < n_past_blk)
        pos = (blk * SEL_BLOCK + row_in_page % SEL_BLOCK).astype(F32)
        s = jnp.where(mask, _bdot(qg, kt) - sl_g * (float(t_pos) - pos), NEG)
        new_ok = jnp.max(jnp.where(idx8[g:g + 1] == cur, 1.0, 0.0), axis=-1, keepdims=True) > 0.5
        s_new = jnp.sum(qg * ksn_ref[0][:, HEAD_DIM * g:HEAD_DIM * (g + 1)], axis=-1, keepdims=True)
        s_new = jnp.where(new_ok, s_new, NEG)
        m = jnp.maximum(jnp.max(s, axis=-1, keepdims=True), s_new)
        e = jnp.where(mask, jnp.exp(s - m), 0.0)
        e_new = jnp.where(new_ok, jnp.exp(s_new - m), 0.0)
        l = jnp.sum(e, axis=-1, keepdims=True) + e_new
        pv = _bdot_nt(e, vt) + e_new * vsn_ref[0][:, HEAD_DIM * g:HEAD_DIM * (g + 1)]
        o_sel.append((pv / jnp.maximum(l, 1e-30))[:GROUP])
    o_sel = jnp.concatenate(o_sel, axis=0)

    w_buf = kwb_ref.shape[1]
    delta = w_buf - lax.broadcasted_iota(I32, (1, w_buf), 1)
    wmask = jnp.broadcast_to((delta <= WINDOW) & (t_pos - delta >= 0), (N_HEADS, w_buf))
    s_win = jnp.where(wmask, _bdot_nt(qr, kwb_ref[0]) - slope * delta.astype(F32), NEG)
    s_wnew = jnp.sum(qr * kwn_ref[0], axis=-1, keepdims=True)
    m = jnp.maximum(jnp.max(s_win, axis=-1, keepdims=True), s_wnew)
    e = jnp.where(wmask, jnp.exp(s_win - m), 0.0)
    e_new = jnp.exp(s_wnew - m)
    l = jnp.sum(e, axis=-1, keepdims=True) + e_new
    o_win = (_bdot(e, vwb_ref[0]) + e_new * vwn_ref[0]) / jnp.maximum(l, 1e-30)

    lane = lax.broadcasted_iota(I32, (N_HEADS, GATE_PAD), 1)
    gn = jnp.broadcast_to(gn_ref[0], (N_HEADS, GATE_PAD))
    gate = lambda j: jnp.sum(jnp.where(lane == 3 * hrow + j, gn, 0.0), axis=-1, keepdims=True)
    o = gate(0) * own_half(ocmp_ref[0]) + gate(1) * o_sel + gate(2) * own_half(o_win)
    o_ref[0] = jnp.concatenate([o[h:h + 1] for h in range(N_HEADS)], axis=1)


def _sample_attend(idx, page_table, q, ks_new, vs_new, kw_buf, vw_buf, kw_new, vw_new, gn, ocmp,
                   ks_pool, vs_pool, t_pos):
    db = q.shape[0]
    w_buf = kw_buf.shape[1]
    n_sel = N_KV_HEADS * SEL_TOP
    n_past_blk = page_table.shape[1] * (PAGE_ROWS // SEL_BLOCK)
    row = lambda n: pl.BlockSpec((1, 1, n), lambda i, a, b: (i, 0, 0))
    grid_spec = pltpu.PrefetchScalarGridSpec(
        num_scalar_prefetch=2,
        grid=(db,),
        in_specs=[row(ATTN_DIM),
                  pl.BlockSpec((1, N_KV_HEADS, SEL_TOP), lambda i, a, b: (i, 0, 0)),
                  row(KV_DIM), row(KV_DIM),
                  pl.BlockSpec((1, w_buf, KV_DIM), lambda i, a, b: (i, 0, 0)),
                  pl.BlockSpec((1, w_buf, KV_DIM), lambda i, a, b: (i, 0, 0)),
                  row(KV_DIM), row(KV_DIM), row(GATE_PAD),
                  pl.BlockSpec((1, N_HEADS, KV_DIM), lambda i, a, b: (i, 0, 0)),
                  pl.BlockSpec(memory_space=pl.ANY), pl.BlockSpec(memory_space=pl.ANY)],
        out_specs=row(ATTN_DIM),
        scratch_shapes=[pltpu.VMEM((2, n_sel, HEAD_DIM, PAGE_ROWS), F32),
                        pltpu.VMEM((2, n_sel, HEAD_DIM, PAGE_ROWS), F32),
                        pltpu.SemaphoreType.DMA((2, 2))],
    )
    return pl.pallas_call(
        functools.partial(_sample_attend_kernel, t_pos=t_pos, n_past_blk=n_past_blk),
        grid_spec=grid_spec,
        out_shape=jax.ShapeDtypeStruct((db, 1, ATTN_DIM), F32),
        compiler_params=_params(("arbitrary",)),
        name="sample_attend",
    )(idx.reshape(-1), page_table, q, idx, ks_new, vs_new, kw_buf, vw_buf, kw_new, vw_new, gn, ocmp,
      ks_pool, vs_pool)


def _merge_kernel(x_ref, oatt_ref, u_ref, hist_ref, bg_ref, ma_ref, mc_ref, cw_ref,
                  wao_ref, wco_ref, wo_ref, gate1_ref, shift2_ref, scale2_ref, g2_ref,
                  x1_ref, h2_ref, *, per_token_history):
    u = u_ref[0]
    tm = u.shape[0]
    if per_token_history:
        u2, u1 = hist_ref[0], hist_ref[1]
    else:
        prev = hist_ref[0]
        first = pl.program_id(1) == 0
        p1 = jnp.where(first, 0.0, prev[7:8])
        p2 = jnp.where(first, 0.0, prev[6:7])
        r = lax.broadcasted_iota(I32, (tm, 1), 0)
        u1 = jnp.where(r == 0, p1, pltpu.roll(u, 1, 0))
        u2 = jnp.where(r == 0, p2, jnp.where(r == 1, p1, pltpu.roll(u, 2, 0)))
    cw = cw_ref[...]
    conv = u2 * cw[0:1] + u1 * cw[1:2] + u * cw[2:3]
    y_conv = bg_ref[0] * conv
    mix = ma_ref[0] * _bdot(oatt_ref[0], wao_ref[...]) + mc_ref[0] * _bdot(y_conv, wco_ref[...])
    x1 = x_ref[0] + gate1_ref[0] * _bdot(mix, wo_ref[...])
    x1_ref[0] = x1
    h2_ref[0] = _rms_mod(x1, g2_ref[...], shift2_ref[0], scale2_ref[0])


def _merge(x, oatt, u, hist, bg, ma, mc, conv_w, wao, wco, wo, gate1, shift2, scale2, g2, tm):
    b, s, d = x.shape
    tm = min(tm, s)
    per_token = hist is not None
    r = gate1.shape[1]
    rb = 1 if r == 1 else tm
    mod_map = (lambda i, j: (i, 0, 0)) if r == 1 else (lambda i, j: (i, j, 0))
    row = lambda n: pl.BlockSpec((1, tm, n), lambda i, j: (i, j, 0))
    const = lambda a: pl.BlockSpec(a.shape, lambda i, j: (0, 0))
    mod = pl.BlockSpec((1, rb, d), mod_map)
    if per_token:
        assert b == 1
        hist_arr = hist
        hist_spec = pl.BlockSpec((2, tm, CONV_DIM), lambda i, j: (0, j, 0))
    else:
        hist_arr = u
        hist_spec = pl.BlockSpec((1, 8, CONV_DIM), lambda i, j: (i, jnp.maximum(j * (tm // 8) - 1, 0), 0))
    return pl.pallas_call(
        functools.partial(_merge_kernel, per_token_history=per_token),
        grid=(b, s // tm),
        in_specs=[row(d), row(ATTN_DIM), row(CONV_DIM), hist_spec, row(CONV_DIM), row(d), row(d),
                  const(conv_w), const(wao), const(wco), const(wo), mod, mod, mod, const(g2)],
        out_specs=[row(d), row(d)],
        out_shape=[jax.ShapeDtypeStruct((b, s, d), F32)] * 2,
        compiler_params=_params(("arbitrary", "arbitrary")),
        name="merge",
    )(x, oatt, u, hist_arr, bg, ma, mc, conv_w, wao, wco, wo, gate1, shift2, scale2, g2)


def _route_kernel(h_ref, rwt_ref, rb_ref, e_ref, w_ref, pos_ref, cnt_ref):
    @pl.when(pl.program_id(0) == 0)
    def _():
        cnt_ref[...] = jnp.zeros_like(cnt_ref)

    tm = h_ref.shape[0]
    per_group = N_EXPERTS // N_GROUPS
    aff = _sigmoid(_bdot_nt(rwt_ref[...], h_ref[...]))
    biased = aff + rb_ref[...]
    row = lax.broadcasted_iota(I32, (N_EXPERTS, tm), 0)
    lrow = lax.broadcasted_iota(I32, (per_group, tm), 0)
    ninf = -jnp.inf

    score = []
    for g in range(N_GROUPS):
        xg = biased[g * per_group:(g + 1) * per_group]
        m1 = jnp.max(xg, axis=0, keepdims=True)
        i1 = jnp.min(jnp.where(xg == m1, lrow, per_group), axis=0, keepdims=True)
        m2 = jnp.max(jnp.where(lrow == i1, ninf, xg), axis=0, keepdims=True)
        score.append(m1 + m2)
    cand = []
    for a in range(N_GROUPS):
        ahead = jnp.zeros((1, tm), F32)
        for b in range(N_GROUPS):
            if b != a:
                wins = (score[b] > score[a]) | ((score[b] == score[a]) & (b < a))
                ahead = ahead + wins.astype(F32)
        cand.append(jnp.where(ahead < TOPK_GROUPS, biased[a * per_group:(a + 1) * per_group], ninf))
    cand = jnp.concatenate(cand, axis=0)

    chosen = jnp.zeros((N_EXPERTS, tm), F32)
    e_rows, w_rows = [], []
    for _ in range(TOP_K):
        m = jnp.max(cand, axis=0, keepdims=True)
        idx = jnp.min(jnp.where(cand == m, row, N_EXPERTS), axis=0, keepdims=True)
        hit = row == idx
        e_rows.append(idx)
        w_rows.append(jnp.sum(jnp.where(hit, aff, 0.0), axis=0, keepdims=True))
        cand = jnp.where(hit, ninf, cand)
        chosen = chosen + hit.astype(F32)
    total = w_rows[0]
    for w in w_rows[1:]:
        total = total + w
    w_rows = [w / total * ROUTED_SCALE for w in w_rows]

    earlier = (lax.broadcasted_iota(I32, (tm, tm), 0) < lax.broadcasted_iota(I32, (tm, tm), 1)).astype(BF16)
    before = _bdot(chosen, earlier) + cnt_ref[...]
    pos_rows = [jnp.sum(jnp.where(row == e, before, 0.0), axis=0, keepdims=True) for e in e_rows]
    cnt_ref[...] += jnp.sum(chosen, axis=1, keepdims=True)

    e_ref[...] = jnp.concatenate(e_rows, axis=0)
    w_ref[...] = jnp.concatenate(w_rows, axis=0)
    pos_ref[...] = jnp.concatenate(pos_rows, axis=0).astype(I32)


def _route(h, rwt, rb, tm):
    n, d = h.shape
    tm = min(tm, n)
    slot = pl.BlockSpec((TOP_K, tm), lambda i: (0, i))
    return pl.pallas_call(
        _route_kernel,
        grid=(n // tm,),
        in_specs=[pl.BlockSpec((tm, d), lambda i: (i, 0)),
                  pl.BlockSpec((N_EXPERTS, d), lambda i: (0, 0)),
                  pl.BlockSpec((N_EXPERTS, 1), lambda i: (0, 0))],
        out_specs=[slot, slot, slot, pl.BlockSpec((N_EXPERTS, 1), lambda i: (0, 0))],
        out_shape=[jax.ShapeDtypeStruct((TOP_K, n), I32), jax.ShapeDtypeStruct((TOP_K, n), F32),
                   jax.ShapeDtypeStruct((TOP_K, n), I32), jax.ShapeDtypeStruct((N_EXPERTS, 1), F32)],
        compiler_params=_params(("arbitrary",)),
        name="moe_route",
    )(h, rwt, rb)


def _dest_kernel(e_ref, pos_ref, start_ref, d_ref):
    tm = e_ref.shape[1]
    row = lax.broadcasted_iota(I32, (N_EXPERTS, tm), 0)
    start = start_ref[...]
    rows = []
    for k in range(TOP_K):
        base = jnp.sum(jnp.where(row == e_ref[k:k + 1], start, 0.0), axis=0, keepdims=True)
        rows.append(base.astype(I32) + pos_ref[k:k + 1])
    d_ref[...] = jnp.concatenate(rows, axis=0)


def _dest(e_t, pos_t, start, tm):
    n = e_t.shape[1]
    tm = min(tm, n)
    slot = pl.BlockSpec((TOP_K, tm), lambda i: (0, i))
    return pl.pallas_call(
        _dest_kernel,
        grid=(n // tm,),
        in_specs=[slot, slot, pl.BlockSpec((N_EXPERTS, 1), lambda i: (0, 0))],
        out_specs=slot,
        out_shape=jax.ShapeDtypeStruct((TOP_K, n), I32),
        compiler_params=_params(("arbitrary",)),
        name="moe_dest",
    )(e_t, pos_t, start)


def _dispatch_kernel(zblk_ref, dest_ref, h_ref, xs_out, zbuf, sem):
    tm = h_ref.shape[0]

    @pl.when(pl.program_id(0) == 0)
    def _():
        zbuf[...] = jnp.zeros_like(zbuf)

        def block_copy(j):
            start = pl.multiple_of(zblk_ref[j] * MOE_ROWS, MOE_ROWS)
            return pltpu.make_async_copy(zbuf, xs_out.at[pl.ds(start, MOE_ROWS)], sem)

        def issue_zero(j, c):
            @pl.when(zblk_ref[j] >= 0)
            def _():
                block_copy(j).start()
            return c

        def drain_zero(j, c):
            @pl.when(zblk_ref[j] >= 0)
            def _():
                block_copy(j).wait()
            return c

        lax.fori_loop(0, zblk_ref.shape[0], issue_zero, 0)
        lax.fori_loop(0, zblk_ref.shape[0], drain_zero, 0)

    def row_copy(n, d):
        return pltpu.make_async_copy(h_ref.at[pl.ds(n, 1)], xs_out.at[pl.ds(d, 1)], sem)

    def issue(n, c):
        for k in range(TOP_K):
            row_copy(n, dest_ref[n * TOP_K + k]).start(priority=k % 2)
        return c

    lax.fori_loop(0, tm, issue, 0)
    for k in range(TOP_K):
        pltpu.make_async_copy(h_ref, xs_out.at[pl.ds(0, tm)], sem).wait()


def _dispatch(zero_blocks, dest_flat, h, n_rows, tm):
    n, d = h.shape
    tm = min(tm, n)
    grid_spec = pltpu.PrefetchScalarGridSpec(
        num_scalar_prefetch=1,
        grid=(n // tm,),
        in_specs=[pl.BlockSpec((tm * TOP_K,), lambda i, zt: (i,), memory_space=pltpu.SMEM),
                  pl.BlockSpec((tm, d), lambda i, zt: (i, 0))],
        out_specs=pl.BlockSpec(memory_space=pl.ANY),
        scratch_shapes=[pltpu.VMEM((MOE_ROWS, d), F32), pltpu.SemaphoreType.DMA(())],
    )
    return pl.pallas_call(
        _dispatch_kernel,
        grid_spec=grid_spec,
        out_shape=jax.ShapeDtypeStruct((n_rows, d), F32),
        compiler_params=_params(("arbitrary",)),
        name="moe_dispatch",
    )(zero_blocks, dest_flat, h)


def _expert_kernel(be_ref, used_ref, x_ref, wg_ref, wu_ref, wd_ref, o_ref):
    i = pl.program_id(0)

    @pl.when(i < used_ref[0])
    def _():
        x = x_ref[...].astype(BF16)
        g = jnp.dot(x, wg_ref[0], preferred_element_type=F32)
        u = jnp.dot(x, wu_ref[0], preferred_element_type=F32)
        o_ref[...] = jnp.dot((_silu(g) * u).astype(BF16), wd_ref[0], preferred_element_type=F32)

    @pl.when(i >= used_ref[0])
    def _():
        o_ref[...] = jnp.zeros_like(o_ref)


def _experts(block_e, n_used, xs, wg, wu, wd):
    n_rows, d = xs.shape
    ff = wg.shape[2]
    grid_spec = pltpu.PrefetchScalarGridSpec(
        num_scalar_prefetch=2,
        grid=(n_rows // MOE_ROWS,),
        in_specs=[pl.BlockSpec((MOE_ROWS, d), lambda i, be, nu: (jnp.minimum(i, nu[0] - 1), 0)),
                  pl.BlockSpec((1, d, ff), lambda i, be, nu: (be[i], 0, 0)),
                  pl.BlockSpec((1, d, ff), lambda i, be, nu: (be[i], 0, 0)),
                  pl.BlockSpec((1, ff, d), lambda i, be, nu: (be[i], 0, 0))],
        out_specs=pl.BlockSpec((MOE_ROWS, d), lambda i, be, nu: (i, 0)),
    )
    return pl.pallas_call(
        _expert_kernel,
        grid_spec=grid_spec,
        out_shape=jax.ShapeDtypeStruct((n_rows, d), F32),
        compiler_params=_params(("arbitrary",)),
        name="moe_experts",
    )(block_e, n_used, xs, wg, wu, wd)


def _combine_kernel(dest_ref, w_ref, x1_ref, h2_ref, gate2_ref, sg_ref, su_ref, sd_ref, ys_hbm,
                    o_ref, gbuf, sem):
    tm = x1_ref.shape[0]

    def row_copy(n, k, d):
        return pltpu.make_async_copy(ys_hbm.at[pl.ds(d, 1)], gbuf.at[k, pl.ds(n, 1)], sem)

    def issue(n, c):
        for k in range(TOP_K):
            row_copy(n, k, dest_ref[n * TOP_K + k]).start(priority=k % 2)
        return c

    lax.fori_loop(0, tm, issue, 0)
    h = h2_ref[...].astype(BF16)
    g = jnp.dot(h, sg_ref[...], preferred_element_type=F32)
    u = jnp.dot(h, su_ref[...], preferred_element_type=F32)
    y = jnp.dot((_silu(g) * u).astype(BF16), sd_ref[...], preferred_element_type=F32)
    for k in range(TOP_K):
        pltpu.make_async_copy(ys_hbm.at[pl.ds(0, tm)], gbuf.at[k], sem).wait()
    w = w_ref[...]
    for k in range(TOP_K):
        y = y + w[:, k:k + 1] * gbuf[k]
    o_ref[...] = x1_ref[...] + gate2_ref[0] * y


def _combine(dest_flat, w, x1, h2, gate2, tokens_per_gate, sg, su, sd, ys, tm):
    n, d = x1.shape
    tm = min(tm, n)
    row = pl.BlockSpec((tm, d), lambda i: (i, 0))
    const = lambda a: pl.BlockSpec(a.shape, lambda i: (0, 0))
    if gate2.shape[1] == 1:
        gate_spec = pl.BlockSpec((1, 1, d), lambda i: ((i * tm) // tokens_per_gate, 0, 0))
    else:
        gate_spec = pl.BlockSpec((1, tm, d), lambda i: (0, i, 0))
    return pl.pallas_call(
        _combine_kernel,
        grid=(n // tm,),
        in_specs=[pl.BlockSpec((tm * TOP_K,), lambda i: (i,), memory_space=pltpu.SMEM),
                  pl.BlockSpec((tm, TOP_K), lambda i: (i, 0)),
                  row, row, gate_spec, const(sg), const(su), const(sd),
                  pl.BlockSpec(memory_space=pl.ANY)],
        out_specs=row,
        out_shape=jax.ShapeDtypeStruct((n, d), F32),
        scratch_shapes=[pltpu.VMEM((TOP_K, tm, d), F32), pltpu.SemaphoreType.DMA(())],
        compiler_params=_params(("arbitrary",)),
        name="moe_combine",
    )(dest_flat, w, x1, h2, gate2, sg, su, sd, ys)


def _moe(h2, x1, gate2, tokens_per_gate, mw):
    n = h2.shape[0]
    e_t, w_t, pos_t, counts = _route(h2, mw["rwt"], mw["rb"], 256)
    counts = counts.reshape(-1).astype(I32)
    padded = (counts + MOE_ROWS - 1) // MOE_ROWS * MOE_ROWS
    pad_end = jnp.cumsum(padded)
    n_blocks = n * TOP_K // MOE_ROWS + N_EXPERTS
    start = (pad_end - padded).astype(F32).reshape(N_EXPERTS, 1)
    dest = _dest(e_t, pos_t, start, 512).T.reshape(-1)
    block_e = jnp.minimum(jnp.searchsorted(pad_end, jnp.arange(n_blocks, dtype=I32) * MOE_ROWS, side="right"),
                          N_EXPERTS - 1).astype(I32)
    n_used = (pad_end[-1:] // MOE_ROWS).astype(I32)
    blocks = jnp.arange(n_blocks, dtype=I32)
    zero_blocks = jnp.concatenate([jnp.where(padded > 0, pad_end // MOE_ROWS - 1, -1).astype(I32),
                                   jnp.where(blocks >= n_used[0], blocks, -1)])
    xs = _dispatch(zero_blocks, dest, h2, n_blocks * MOE_ROWS, 256)
    ys = _experts(block_e, n_used, xs, mw["wg"], mw["wu"], mw["wd"])
    return _combine(dest, w_t.T, x1, h2, gate2, tokens_per_gate, mw["sg"], mw["su"], mw["sd"], ys, 128)


def _pack_w_in(w_in):
    n_gate = N_HEADS * 3
    kv0 = ATTN_DIM
    g0 = kv0 + 6 * KV_DIM
    w_rows = jnp.concatenate([w_in[:, kv0:g0], w_in[:, g0 + n_gate:]], axis=1).astype(BF16)
    gate = jnp.pad(w_in[:, g0:g0 + n_gate], ((0, 0), (0, GATE_ROWS - n_gate)))
    v_sel = w_in[:, kv0 + 3 * KV_DIM:kv0 + 4 * KV_DIM]
    v_win = w_in[:, kv0 + 5 * KV_DIM:kv0 + 6 * KV_DIM]
    w_feat = jnp.concatenate([w_in[:, :ATTN_DIM], v_sel, v_win, gate], axis=1).T.astype(BF16)
    return w_rows, w_feat


def _compress_weights(cmp_pe, cmp_w1, cmp_w2, g_kcmp):
    eye = jnp.eye(N_KV_HEADS, dtype=F32)
    sub = CMP_LEN // CMP_STRIDE

    def w1_cat(w1):
        w = w1.reshape(sub, CMP_STRIDE, HEAD_DIM, CMP_HID)
        mats = [jnp.einsum("sdh,gk->sgdkh", w[j], eye).reshape(CMP_STRIDE * KV_DIM, N_KV_HEADS * CMP_HID)
                for j in range(sub)]
        return jnp.concatenate(mats, axis=1).astype(BF16)

    def w2_blk(w2):
        return jnp.einsum("hd,gk->ghkd", w2, eye).reshape(N_KV_HEADS * CMP_HID, KV_DIM).astype(BF16)

    def pe_rows(pe):
        p = pe.reshape(sub, CMP_STRIDE, 1, HEAD_DIM)
        p = jnp.broadcast_to(p, (sub, CMP_STRIDE, N_KV_HEADS, HEAD_DIM)).reshape(sub, 1, CMP_STRIDE * KV_DIM)
        return jnp.broadcast_to(p, (sub, 8, CMP_STRIDE * KV_DIM))

    assert sub == 2
    return {
        "w1k": w1_cat(cmp_w1[0]), "w1v": w1_cat(cmp_w1[1]),
        "w2k": w2_blk(cmp_w2[0]), "w2v": w2_blk(cmp_w2[1]),
        "pe": jnp.concatenate([pe_rows(cmp_pe[0]), pe_rows(cmp_pe[1])], axis=0),
        "gk": jnp.tile(g_kcmp, N_KV_HEADS).reshape(1, KV_DIM),
    }


def _layer(l, x_p, x_s, c_all, caches, page_table, w):
    ckc, cvc, cks, cvs, ckw, cvw, sconv = caches
    b, s, d = x_p.shape
    db, ds = x_s.shape[:2]
    assert ds == 1, "one new token per sample sequence"
    n_pool = ckc.shape[1]
    t_pos = page_table.shape[1] * PAGE_ROWS
    hist = CONV_WIDTH - 1

    ada = _ada(c_all, w["w_ada"][l], w["b_ada"][l])
    mods = [ada[:, k * d:(k + 1) * d] for k in range(6)]
    mod_p = [m[:b, None] for m in mods]
    mod_s = [m[b:][None] for m in mods]

    qkg = w["qk_norm_g"][l]
    gq = jnp.tile(qkg[0], N_HEADS).reshape(-1, 1)
    gks = jnp.tile(qkg[2], N_KV_HEADS).reshape(1, -1)
    gkw = jnp.tile(qkg[3], N_KV_HEADS).reshape(1, -1)
    g1 = w["norm1_g"][l].reshape(1, d)
    g2 = w["norm2_g"][l].reshape(1, d)
    w_rows, w_feat = _pack_w_in(w["w_in"][l])
    cw = _compress_weights(w["cmp_pe"][l], w["cmp_w1"][l], w["cmp_w2"][l], qkg[1])
    wao, wco, wo = (w[k][l].astype(BF16) for k in ("w_attn_out", "w_conv_out", "w_o"))
    mw = {"rwt": w["router_w"][l].T.astype(BF16), "rb": w["router_b"][l].reshape(-1, 1),
          "wg": w["exp_w_gate"][l].astype(BF16), "wu": w["exp_w_up"][l].astype(BF16),
          "wd": w["exp_w_down"][l].astype(BF16), "sg": w["shared_w_gate"][l].astype(BF16),
          "su": w["shared_w_up"][l].astype(BF16), "sd": w["shared_w_down"][l].astype(BF16)}
    pages = lambda a: a.reshape(-1, PAGE_ROWS, KV_DIM)
    feature_major = lambda a: jnp.transpose(a, (0, 2, 3, 1))

    (qt, kc, vc, ks, vs, kw, vw, ksb, kwb, vst, vwt, gnt, u, bg, ma, mc) = _in_proj(
        x_p, mod_p[0], mod_p[1], g1, w_rows, w_feat, gq, gks, gkw, 256)
    own_pages = jnp.arange(b * s // PAGE_ROWS, dtype=I32).reshape(b, s // PAGE_ROWS)
    kcmp, vcmp = _compress(own_pages, pages(kc), pages(vc), cw)
    o_att = _nsa_prompt(qt, gnt, kcmp, vcmp, ksb, vst, kwb, vwt, 256, 256)
    x1, h2 = _merge(x_p, o_att, u, None, bg, ma, mc, w["conv_w"][l], wao, wco, wo,
                    mod_p[2], mod_p[3], mod_p[4], g2, 256)
    y_p = _moe(h2.reshape(b * s, d), x1.reshape(b * s, d), mod_p[5], s, mw).reshape(b, s, d)
    keep = min(WINDOW, s)
    heads = lambda a: a.reshape(a.shape[0], a.shape[1], N_KV_HEADS, HEAD_DIM)
    p_state = (heads(kc), heads(vc), heads(ks), heads(vs), heads(kw[:, -keep:]), heads(vw[:, -keep:]),
               u[:, -hist:])

    (qt, kc, vc, ks, vs, kw, vw, _, _, _, _, gnt, u, bg, ma, mc) = _in_proj(
        x_s.reshape(1, db, d), mod_s[0], mod_s[1], g1, w_rows, w_feat, gq, gks, gkw, 128)
    per_seq = lambda a: a.reshape(db, 1, a.shape[-1])
    token_major = lambda a: jnp.swapaxes(a[0], 1, 2).reshape(db, a.shape[2]).astype(F32)
    q = token_major(qt)
    gn = jnp.pad(token_major(gnt), ((0, 0), (0, GATE_PAD - GATE_ROWS)))
    kcmp, vcmp = _compress(page_table, pages(ckc[l]), pages(cvc[l]), cw)
    o_cmp, idx = _sample_select(per_seq(q), kcmp, vcmp, t_pos)
    w_buf = ckw.shape[2]
    o_att = _sample_attend(idx, page_table, per_seq(q), per_seq(ks), per_seq(vs),
                           ckw[l].reshape(db, w_buf, KV_DIM), cvw[l].reshape(db, w_buf, KV_DIM),
                           per_seq(kw), per_seq(vw), per_seq(gn), o_cmp, feature_major(cks[l]),
                           feature_major(cvs[l]), t_pos)
    x1, h2 = _merge(x_s.reshape(1, db, d), o_att.reshape(1, db, ATTN_DIM), u, jnp.swapaxes(sconv[l], 0, 1),
                    bg, ma, mc, w["conv_w"][l], wao, wco, wo, mod_s[2], mod_s[3], mod_s[4], g2, 128)
    y_s = _moe(h2.reshape(db, d), x1.reshape(db, d), mod_s[5], 1, mw).reshape(db, 1, d)
    new_row = lambda a: a.reshape(db, 1, N_KV_HEADS, HEAD_DIM)
    s_state = (new_row(kc), new_row(vc), new_row(ks), new_row(vs),
               jnp.concatenate([ckw[l], new_row(kw)], axis=1)[:, -w_buf:],
               jnp.concatenate([cvw[l], new_row(vw)], axis=1)[:, -w_buf:],
               jnp.concatenate([sconv[l], u.reshape(db, 1, CONV_DIM)], axis=1)[:, -hist:])
    return y_p, y_s, p_state + s_state


def kernel(x_prompt, x_sample, cache_k_cmp, cache_v_cmp, cache_k_sel, cache_v_sel, cache_k_win, cache_v_win,
           state_conv, page_table, c_prompt, c_sample, w_ada, b_ada, norm1_g, norm2_g, w_in, qk_norm_g,
           cmp_pe, cmp_w1, cmp_w2, conv_w, w_attn_out, w_conv_out, w_o, router_w, router_b,
           exp_w_gate, exp_w_up, exp_w_down, shared_w_gate, shared_w_up, shared_w_down):
    w = dict(w_ada=w_ada, b_ada=b_ada, norm1_g=norm1_g, norm2_g=norm2_g, w_in=w_in, qk_norm_g=qk_norm_g,
             cmp_pe=cmp_pe, cmp_w1=cmp_w1, cmp_w2=cmp_w2, conv_w=conv_w, w_attn_out=w_attn_out,
             w_conv_out=w_conv_out, w_o=w_o, router_w=router_w, router_b=router_b, exp_w_gate=exp_w_gate,
             exp_w_up=exp_w_up, exp_w_down=exp_w_down, shared_w_gate=shared_w_gate, shared_w_up=shared_w_up,
             shared_w_down=shared_w_down)
    caches = (cache_k_cmp, cache_v_cmp, cache_k_sel, cache_v_sel, cache_k_win, cache_v_win, state_conv)
    c_all = jnp.concatenate([c_prompt, c_sample], axis=0)
    x_p, x_s = x_prompt, x_sample
    states = []
    for l in range(w_ada.shape[0]):
        x_p, x_s, st = _layer(l, x_p, x_s, c_all, caches, page_table.astype(I32), w)
        states.append(st)
    return (x_p, x_s) + tuple(jnp.stack(s) for s in zip(*states))
```
